```python
import math
import jax, jax.numpy as jnp
from jax import lax
import numpy as np

D_MODEL = 1024
BATCH = 2
SEQ = 16384
DEPTH = 4
DEC_BATCH = 32
DEC_SEQ = 2048
PAST_LEN = 128

GRID_W = 64
N_MIXERS = 2
GLA_HEADS = 4
GLA_DK = D_MODEL // 2
GLA_DV = D_MODEL
GLA_DK_HEAD = GLA_DK // GLA_HEADS
GLA_DV_HEAD = GLA_DV // GLA_HEADS
GLA_GATE_RANK = 16
GLA_TAU = 16.0
GLA_CHUNK = 64
GLA_IN = 2 * GLA_DK + 2 * GLA_DV + 2 * GLA_GATE_RANK
NA_HEADS = 16
NA_HEAD_DIM = D_MODEL // NA_HEADS
WIN_ROWS = 8
WIN_COLS = 16
D_FF = 7 * D_MODEL // 2
N_EXPERTS = 8
TOP_K = 2
EXPERT_BLOCK = 256
N_A = (DEPTH + 1) // 2
N_B = DEPTH // 2
NORM_EPS = 1e-6
NEG_INF = -1e30

kernel_name = "hybrid_gla_natten_moe_encoder"


def _rmsnorm(x, gain):
    xf = x.astype(jnp.float32)
    y = xf * lax.rsqrt(jnp.mean(xf * xf, axis=-1, keepdims=True) + NORM_EPS)
    return (y * gain.astype(jnp.float32)).astype(x.dtype)


def _gla_direction(q, k, v, g, include_diag):
    B, L, H, dk = q.shape
    dv = v.shape[-1]
    C = GLA_CHUNK
    N = L // C
    q = q.reshape(B, N, C, H, dk)
    k = k.reshape(B, N, C, H, dk)
    v = v.reshape(B, N, C, H, dv)
    G = jnp.cumsum(g.astype(jnp.float32).reshape(B, N, C, H, dk), axis=2)
    G_ref = G[:, :, C // 2:C // 2 + 1]
    G_last = G[:, :, C - 1:]
    q_rel = q * jnp.exp(G - G_ref)
    k_rel = k * jnp.exp(G_ref - G)
    scores = jnp.einsum('bnihd,bnjhd->bnhij', q_rel, k_rel)
    idx = jnp.arange(C)
    if include_diag:
        allowed = idx[:, None] >= idx[None, :]
    else:
        allowed = idx[:, None] > idx[None, :]
    scores = jnp.where(allowed, scores, 0.0)
    o_intra = jnp.einsum('bnhij,bnjhe->bnihe', scores, v)
    q_in = q * jnp.exp(G)
    k_out = k * jnp.exp(G_last - G)
    decay = jnp.exp(G_last[:, :, 0])

    def step(S, xs):
        qc, kc, vc, dc = xs
        o = jnp.einsum('bihd,bhde->bihe', qc, S)
        S = dc[..., None] * S + jnp.einsum('bjhd,bjhe->bhde', kc, vc)
        return S, o

    S0 = jnp.zeros((B, H, dk, dv), jnp.float32)
    _, o_inter = lax.scan(step, S0, (jnp.moveaxis(q_in, 1, 0), jnp.moveaxis(k_out, 1, 0),
                                     jnp.moveaxis(v, 1, 0), jnp.moveaxis(decay, 1, 0)))
    o = o_intra + jnp.moveaxis(o_inter, 0, 1)
    return o.reshape(B, L, H, dv)


def _gla_mixer(h, w_in, w_gate_up, b_gate, norm_gain, w_out):
    B, L, _ = h.shape
    proj = h @ w_in
    splits = [GLA_DK, 2 * GLA_DK, 2 * GLA_DK + GLA_DV, 2 * GLA_DK + 2 * GLA_DV,
              2 * GLA_DK + 2 * GLA_DV + GLA_GATE_RANK]
    q, k, v, r, gd_f, gd_b = jnp.split(proj, splits, axis=-1)
    shape_k = (B, L, GLA_HEADS, GLA_DK_HEAD)
    q = (q * GLA_DK_HEAD ** -0.5).reshape(shape_k)
    k = k.reshape(shape_k)
    v = v.reshape(B, L, GLA_HEADS, GLA_DV_HEAD)
    g_f = (jax.nn.log_sigmoid((gd_f @ w_gate_up[0] + b_gate[0]).astype(jnp.float32)) / GLA_TAU).reshape(shape_k)
    g_b = (jax.nn.log_sigmoid((gd_b @ w_gate_up[1] + b_gate[1]).astype(jnp.float32)) / GLA_TAU).reshape(shape_k)
    o_fwd = _gla_direction(q, k, v, g_f, True)
    flip = lambda t: jnp.flip(t, axis=1)
    o_bwd = flip(_gla_direction(flip(q), flip(k), flip(v), flip(g_b), False))
    o = o_fwd + o_bwd
    o = o * lax.rsqrt(jnp.mean(o * o, axis=-1, keepdims=True) + NORM_EPS)
    o = o * norm_gain.astype(jnp.float32).reshape(GLA_HEADS, GLA_DV_HEAD)
    o = o.reshape(B, L, GLA_DV).astype(h.dtype) * jax.nn.silu(r)
    return o @ w_out


def _neighbourhood_attention(h, w_qkv, rpb, w_out):
    B, L, _ = h.shape
    rows = L // GRID_W
    kr = min(WIN_ROWS, rows)
    qkv = (h @ w_qkv).reshape(B, rows, GRID_W, 3, NA_HEADS, NA_HEAD_DIM)
    q = qkv[:, :, :, 0] * NA_HEAD_DIM ** -0.5
    k = qkv[:, :, :, 1]
    v = qkv[:, :, :, 2]
    cols = jnp.arange(GRID_W)
    col_start = jnp.clip(cols - WIN_COLS // 2, 0, GRID_W - WIN_COLS)
    in_win = (cols[None, :] >= col_start[:, None]) & (cols[None, :] < col_start[:, None] + WIN_COLS)
    col_mask = jnp.where(in_win, 0.0, NEG_INF).astype(jnp.float32)
    dc_idx = jnp.clip(cols[None, :] - cols[:, None] + WIN_COLS - 1, 0, 2 * WIN_COLS - 2)
    rpb_cols = rpb.astype(jnp.float32)[:, :, dc_idx]

    def row_block(args):
        r, q_row = args
        r_start = jnp.clip(r - WIN_ROWS // 2, 0, rows - kr)
        k_band = lax.dynamic_slice_in_dim(k, r_start, kr, axis=1)
        v_band = lax.dynamic_slice_in_dim(v, r_start, kr, axis=1)
        dr_idx = r_start - r + jnp.arange(kr) + WIN_ROWS - 1
        bias = jnp.take(rpb_cols, dr_idx, axis=1)
        s = jnp.einsum('bqhd,brkhd->bhqrk', q_row, k_band).astype(jnp.float32)
        s = s + jnp.transpose(bias, (0, 2, 1, 3))[None] + col_mask[None, None, :, None, :]
        p = jax.nn.softmax(s.reshape(B, NA_HEADS, GRID_W, kr * GRID_W), axis=-1).reshape(s.shape)
        return jnp.einsum('bhqrk,brkhd->bqhd', p.astype(v_band.dtype), v_band)

    o = lax.map(row_block, (jnp.arange(rows), jnp.moveaxis(q, 1, 0)))
    o = jnp.moveaxis(o, 0, 1).reshape(B, L, D_MODEL)
    return o @ w_out


def _dense_swiglu(h, w_gate, w_up, w_down):
    return (jax.nn.silu(h @ w_gate) * (h @ w_up)) @ w_down


def _moe_swiglu(h, w_router, w_gate, w_up, w_down):
    B, L, D = h.shape
    T = B * L
    xf = h.reshape(T, D)
    logits = (xf @ w_router).astype(jnp.float32)
    top_val, top_idx = lax.top_k(logits, TOP_K)
    gates = jax.nn.softmax(top_val, axis=-1)
    n_assign = T * TOP_K
    flat_e = top_idx.reshape(-1)
    flat_tok = jnp.arange(n_assign) // TOP_K
    flat_g = gates.reshape(-1)
    order = jnp.argsort(flat_e)
    se = flat_e[order]
    counts = jnp.bincount(flat_e, length=N_EXPERTS)
    padded = ((counts + EXPERT_BLOCK - 1) // EXPERT_BLOCK) * EXPERT_BLOCK
    start = jnp.cumsum(counts) - counts
    pend = jnp.cumsum(padded)
    pstart = pend - padded
    dest = pstart[se] + (jnp.arange(n_assign) - start[se])
    n_blocks = (n_assign + EXPERT_BLOCK - 1) // EXPERT_BLOCK + N_EXPERTS
    P = n_blocks * EXPERT_BLOCK
    tok_buf = jnp.zeros((P,), jnp.int32).at[dest].set(flat_tok[order])
    gate_buf = jnp.zeros((P,), jnp.float32).at[dest].set(flat_g[order])
    block_e = jnp.minimum(jnp.searchsorted(pend, jnp.arange(n_blocks) * EXPERT_BLOCK, side='right'),
                          N_EXPERTS - 1)
    xb = xf[tok_buf].reshape(n_blocks, EXPERT_BLOCK, D)

    def expert_block(args):
        x_blk, e = args
        return (jax.nn.silu(x_blk @ w_gate[e]) * (x_blk @ w_up[e])) @ w_down[e]

    yb = lax.map(expert_block, (xb, block_e)).reshape(P, D)
    yb = yb * gate_buf[:, None].astype(yb.dtype)
    y = jnp.zeros((T, D), yb.dtype).at[tok_buf].add(yb)
    return y.reshape(B, L, D)


def _trunk(x, c, weights):
    (ada_w, ada_b, norm_gains, gla_w_in, gla_w_gate_up, gla_b_gate, gla_norm, gla_w_out,
     na_w_qkv, na_rpb, na_w_out, ffn_w_gate, ffn_w_up, ffn_w_down,
     moe_router, moe_w_gate, moe_w_up, moe_w_down) = weights
    cs = jax.nn.silu(c)
    for i in range(DEPTH):
        j = i // 2
        mod = (cs @ ada_w[i] + ada_b[i]).reshape(c.shape[0], 6, 1, D_MODEL)
        shift_m, scale_m, gate_m = mod[:, 0], mod[:, 1], mod[:, 2]
        shift_f, scale_f, gate_f = mod[:, 3], mod[:, 4], mod[:, 5]
        h = _rmsnorm(x, norm_gains[i, 0]) * (1.0 + scale_m) + shift_m
        if i % N_MIXERS == 0:
            y = _gla_mixer(h, gla_w_in[j], gla_w_gate_up[j], gla_b_gate[j], gla_norm[j], gla_w_out[j])
        else:
            y = _neighbourhood_attention(h, na_w_qkv[j], na_rpb[j], na_w_out[j])
        x = x + gate_m * _rmsnorm(y, norm_gains[i, 1])
        h = _rmsnorm(x, norm_gains[i, 2]) * (1.0 + scale_f) + shift_f
        if i % 2 == 0:
            y = _dense_swiglu(h, ffn_w_gate[j], ffn_w_up[j], ffn_w_down[j])
        else:
            y = _moe_swiglu(h, moe_router[j], moe_w_gate[j], moe_w_up[j], moe_w_down[j])
        x = x + gate_f * _rmsnorm(y, norm_gains[i, 3])
    return x


def setup_inputs(seed: int = 0) -> dict:
    key = jax.random.key(seed)
    ks = jax.random.split(key, 22)
    D, F, E, R = D_MODEL, D_FF, N_EXPERTS, GLA_GATE_RANK

    def nrm(k, shape, scale):
        return jax.random.normal(k, shape, jnp.float32) * scale

    return {
        'x_prompt': nrm(ks[0], (BATCH, SEQ, D), 1.0),
        'x_sample': nrm(ks[1], (DEC_BATCH, DEC_SEQ, D), 1.0),
        'c_prompt': nrm(ks[2], (BATCH, D), 1.0),
        'c_sample': nrm(ks[3], (DEC_BATCH, D), 1.0),
        'ada_w': nrm(ks[4], (DEPTH, D, 6 * D), 0.5 * D ** -0.5),
        'ada_b': nrm(ks[5], (DEPTH, 6 * D), 0.01),
        'norm_gains': 1.0 + nrm(ks[6], (DEPTH, 4, D), 0.02),
        'gla_w_in': nrm(ks[7], (N_A, D, GLA_IN), D ** -0.5),
        'gla_w_gate_up': nrm(ks[8], (N_A, 2, R, GLA_DK), R ** -0.5),
        'gla_b_gate': nrm(ks[9], (N_A, 2, GLA_DK), 0.1),
        'gla_norm': 1.0 + nrm(ks[10], (N_A, GLA_DV), 0.02),
        'gla_w_out': nrm(ks[11], (N_A, GLA_DV, D), GLA_DV ** -0.5),
        'na_w_qkv': nrm(ks[12], (N_B, D, 3 * D), D ** -0.5),
        'na_rpb': nrm(ks[13], (N_B, NA_HEADS, 2 * WIN_ROWS - 1, 2 * WIN_COLS - 1), 0.1),
        'na_w_out': nrm(ks[14], (N_B, D, D), D ** -0.5),
        'ffn_w_gate': nrm(ks[15], (N_A, D, F), D ** -0.5),
        'ffn_w_up': nrm(ks[16], (N_A, D, F), D ** -0.5),
        'ffn_w_down': nrm(ks[17], (N_A, F, D), F ** -0.5),
        'moe_router': nrm(ks[18], (N_B, D, E), D ** -0.5),
        'moe_w_gate': nrm(ks[19], (N_B, E, D, F), D ** -0.5),
        'moe_w_up': nrm(ks[20], (N_B, E, D, F), D ** -0.5),
        'moe_w_down': nrm(ks[21], (N_B, E, F, D), F ** -0.5),
    }


def reference(x_prompt, x_sample, c_prompt, c_sample, ada_w, ada_b, norm_gains,
              gla_w_in, gla_w_gate_up, gla_b_gate, gla_norm, gla_w_out,
              na_w_qkv, na_rpb, na_w_out, ffn_w_gate, ffn_w_up, ffn_w_down,
              moe_router, moe_w_gate, moe_w_up, moe_w_down):
    weights = (ada_w, ada_b, norm_gains, gla_w_in, gla_w_gate_up, gla_b_gate, gla_norm, gla_w_out,
               na_w_qkv, na_rpb, na_w_out, ffn_w_gate, ffn_w_up, ffn_w_down,
               moe_router, moe_w_gate, moe_w_up, moe_w_down)
    y_prompt = _trunk(x_prompt, c_prompt, weights)
    y_sample = _trunk(x_sample, c_sample, weights)
    return (y_prompt, y_sample)
```

```python
import functools

import jax
import jax.numpy as jnp
from jax import lax
from jax.experimental import pallas as pl
from jax.experimental.pallas import tpu as pltpu

D_MODEL = 1024
DEPTH = 4
GRID_W = 64
GLA_HEADS = 4
GLA_DK = D_MODEL // 2
GLA_DV = D_MODEL
GLA_DK_HEAD = GLA_DK // GLA_HEADS
GLA_DV_HEAD = GLA_DV // GLA_HEADS
GLA_GATE_RANK = 16
GLA_TAU = 16.0
GLA_CHUNK = 64
NA_HEADS = 16
NA_HEAD_DIM = D_MODEL // NA_HEADS
WIN_ROWS = 8
WIN_COLS = 16
D_FF = 7 * D_MODEL // 2
N_EXPERTS = 8
TOP_K = 2
NORM_EPS = 1e-6
NEG_INF = -1e30

BF16 = jnp.bfloat16
F32 = jnp.float32

VMEM_LIMIT_BYTES = 56 * 1024 * 1024
LANES = 128

ROW_TILE = 512
FFN_ROW_TILE = 1024
FFN_COL_TILE = 512
GLA_TIME_BLOCK = 512
NA_Q_ROWS = 4
NA_K_ROWS = 12
MOE_BLOCK = 1024


def _cparams(*sem):
    return pltpu.CompilerParams(dimension_semantics=sem, vmem_limit_bytes=VMEM_LIMIT_BYTES)


def _norm_mod(x, gain, scale, shift):
    ms = jnp.mean(x * x, axis=-1, keepdims=True)
    y = x * lax.rsqrt(ms + NORM_EPS) * gain
    return y * (1.0 + scale) + shift


def _rms_gain(y, gain):
    ms = jnp.mean(y * y, axis=-1, keepdims=True)
    return y * lax.rsqrt(ms + NORM_EPS) * gain


def _dot(a, b):
    return jnp.dot(a, b, preferred_element_type=F32)


def _dot_nt(a, b):
    return lax.dot_general(a, b, (((1,), (1,)), ((), ())), preferred_element_type=F32)


def _dot_tn(a, b):
    return lax.dot_general(a, b, (((0,), (0,)), ((), ())), preferred_element_type=F32)


def _silu(x):
    return x * (1.0 / (1.0 + jnp.exp(-x)))


def _small_matmul_kernel(a_ref, w_ref, b_ref, o_ref, *, act):
    a = a_ref[...]
    if act:
        a = _silu(a)
    o_ref[0] = jnp.dot(a, w_ref[0], preferred_element_type=F32,
                       precision=lax.Precision.HIGHEST) + b_ref[0]


def _small_matmul(a, w, b, *, act, col_tile):
    M, K = a.shape
    G, _, N = w.shape
    return pl.pallas_call(
        functools.partial(_small_matmul_kernel, act=act),
        grid=(G, N // col_tile),
        in_specs=[pl.BlockSpec((M, K), lambda g, j: (0, 0)),
                  pl.BlockSpec((1, K, col_tile), lambda g, j: (g, 0, j)),
                  pl.BlockSpec((1, 1, col_tile), lambda g, j: (g, 0, j))],
        out_specs=pl.BlockSpec((1, M, col_tile), lambda g, j: (g, 0, j)),
        out_shape=jax.ShapeDtypeStruct((G, M, N), F32),
        compiler_params=_cparams("parallel", "parallel"),
        name="small_matmul",
    )(a, w, b)


def _proj_kernel(x_ref, g_ref, mod_ref, w_ref, zb_ref, o_ref, z_ref, *, n_main, n_z, cn):
    m = mod_ref[0]
    h = _norm_mod(x_ref[...], g_ref[...], m[1:2], m[0:1]).astype(BF16)
    for n in range(0, n_main, cn):
        o_ref[:, n:n + cn] = _dot(h, w_ref[:, n:n + cn]).astype(o_ref.dtype)
    for n in range(0, n_z, cn):
        z_ref[:, n:n + cn] = _dot(h, w_ref[:, n_main + n:n_main + n + cn]) + zb_ref[:, n:n + cn]


def _proj_kernel_noz(x_ref, g_ref, mod_ref, w_ref, o_ref, *, n_main, cn):
    m = mod_ref[0]
    h = _norm_mod(x_ref[...], g_ref[...], m[1:2], m[0:1]).astype(BF16)
    for n in range(0, n_main, cn):
        o_ref[:, n:n + cn] = _dot(h, w_ref[:, n:n + cn]).astype(o_ref.dtype)


def _proj(x, gain, mod, w, zb, L, n_main, n_z):
    T = x.shape[0]
    tm = min(ROW_TILE, L)
    spt = L // tm
    N = n_main + n_z
    in_specs = [pl.BlockSpec((tm, D_MODEL), lambda i: (i, 0)),
                pl.BlockSpec((1, D_MODEL), lambda i: (0, 0)),
                pl.BlockSpec((1, 6, D_MODEL), lambda i: (i // spt, 0, 0)),
                pl.BlockSpec((D_MODEL, N), lambda i: (0, 0))]
    if n_z:
        return pl.pallas_call(
            functools.partial(_proj_kernel, n_main=n_main, n_z=n_z, cn=512),
            grid=(T // tm,),
            in_specs=in_specs + [pl.BlockSpec((1, n_z), lambda i: (0, 0))],
            out_specs=[pl.BlockSpec((tm, n_main), lambda i: (i, 0)),
                       pl.BlockSpec((tm, n_z), lambda i: (i, 0))],
            out_shape=[jax.ShapeDtypeStruct((T, n_main), BF16),
                       jax.ShapeDtypeStruct((T, n_z), F32)],
            compiler_params=_cparams("parallel"),
            name="norm_proj_gla",
        )(x, gain, mod, w, zb)
    return pl.pallas_call(
        functools.partial(_proj_kernel_noz, n_main=n_main, cn=512),
        grid=(T // tm,),
        in_specs=in_specs,
        out_specs=pl.BlockSpec((tm, n_main), lambda i: (i, 0)),
        out_shape=jax.ShapeDtypeStruct((T, n_main), BF16),
        compiler_params=_cparams("parallel"),
        name="norm_proj_na",
    )(x, gain, mod, w)


def _outproj_kernel(o_ref, w_ref, x_ref, g_ref, mod_ref, y_ref):
    y = _dot(o_ref[...], w_ref[...])
    gate = mod_ref[0][2:3]
    y_ref[...] = x_ref[...] + gate * _rms_gain(y, g_ref[...])


def _outproj(o, w, x, gain, mod, L):
    T, K = o.shape
    tm = min(ROW_TILE, L)
    spt = L // tm
    return pl.pallas_call(
        _outproj_kernel,
        grid=(T // tm,),
        in_specs=[pl.BlockSpec((tm, K), lambda i: (i, 0)),
                  pl.BlockSpec((K, D_MODEL), lambda i: (0, 0)),
                  pl.BlockSpec((tm, D_MODEL), lambda i: (i, 0)),
                  pl.BlockSpec((1, D_MODEL), lambda i: (0, 0)),
                  pl.BlockSpec((1, 6, D_MODEL), lambda i: (i // spt, 0, 0))],
        out_specs=pl.BlockSpec((tm, D_MODEL), lambda i: (i, 0)),
        out_shape=jax.ShapeDtypeStruct((T, D_MODEL), F32),
        compiler_params=_cparams("parallel"),
        name="outproj_residual",
    )(o, w, x, gain, mod)


def _ffn_kernel(x_ref, g2_ref, g3_ref, mod_ref, wg_ref, wu_ref, wd_ref, y_ref, h_ref, acc_ref):
    f = pl.program_id(1)
    m = mod_ref[0]

    @pl.when(f == 0)
    def _():
        h_ref[...] = _norm_mod(x_ref[...], g2_ref[...], m[4:5], m[3:4]).astype(BF16)

    h = h_ref[...]
    a = (_silu(_dot(h, wg_ref[...])) * _dot(h, wu_ref[...])).astype(BF16)
    part = _dot(a, wd_ref[...])

    @pl.when(f == 0)
    def _():
        acc_ref[...] = part

    @pl.when(f > 0)
    def _():
        acc_ref[...] += part

    @pl.when(f == pl.num_programs(1) - 1)
    def _():
        y_ref[...] = x_ref[...] + m[5:6] * _rms_gain(acc_ref[...], g3_ref[...])


def _ffn(x, gain2, gain3, mod, wg, wu, wd, L):
    T = x.shape[0]
    tm = min(FFN_ROW_TILE, L)
    spt = L // tm
    tf = FFN_COL_TILE
    return pl.pallas_call(
        _ffn_kernel,
        grid=(T // tm, D_FF // tf),
        in_specs=[pl.BlockSpec((tm, D_MODEL), lambda i, f: (i, 0)),
                  pl.BlockSpec((1, D_MODEL), lambda i, f: (0, 0)),
                  pl.BlockSpec((1, D_MODEL), lambda i, f: (0, 0)),
                  pl.BlockSpec((1, 6, D_MODEL), lambda i, f: (i // spt, 0, 0)),
                  pl.BlockSpec((D_MODEL, tf), lambda i, f: (0, f)),
                  pl.BlockSpec((D_MODEL, tf), lambda i, f: (0, f)),
                  pl.BlockSpec((tf, D_MODEL), lambda i, f: (f, 0))],
        out_specs=pl.BlockSpec((tm, D_MODEL), lambda i, f: (i, 0)),
        out_shape=jax.ShapeDtypeStruct((T, D_MODEL), F32),
        scratch_shapes=[pltpu.VMEM((tm, D_MODEL), BF16), pltpu.VMEM((tm, D_MODEL), F32)],
        compiler_params=_cparams("parallel", "arbitrary"),
        name="ffn_swiglu",
    )(x, gain2, gain3, mod, wg, wu, wd)


def _gla_chunk(q, k, v, z, S, *, reverse):
    C = GLA_CHUNK
    g = (jnp.minimum(z, 0.0) - jnp.log(1.0 + jnp.exp(-jnp.abs(z)))) * (1.0 / GLA_TAU)
    row = lax.broadcasted_iota(jnp.int32, (C, C), 0)
    col = lax.broadcasted_iota(jnp.int32, (C, C), 1)
    if reverse:
        tri = (col >= row).astype(F32)
        allowed = col > row
        i_ref, i_last = C - 1 - C // 2, 0
    else:
        tri = (row >= col).astype(F32)
        allowed = row >= col
        i_ref, i_last = C // 2, C - 1
    G = jnp.dot(tri, g, preferred_element_type=F32, precision=lax.Precision.HIGHEST)
    G_ref = G[i_ref:i_ref + 1]
    G_last = G[i_last:i_last + 1]
    q_rel = (q * jnp.exp(G - G_ref)).astype(BF16)
    k_rel = (k * jnp.exp(G_ref - G)).astype(BF16)
    scores = jnp.where(allowed, _dot_nt(q_rel, k_rel), 0.0)
    o = _dot(scores.astype(BF16), v)
    o = o + _dot((q * jnp.exp(G)).astype(BF16), S.astype(BF16))
    k_out = (k * jnp.exp(G_last - G)).astype(BF16)
    kv = _dot_tn(k_out, v)
    dk = G.shape[1]
    decay = jnp.broadcast_to(jnp.exp(G_last), (dk, dk)).T
    reps = S.shape[1] // dk
    S_new = jnp.concatenate([decay] * reps, axis=1) * S + kv
    return o, S_new


def _gla_fwd_kernel(q_ref, k_ref, v_ref, z_ref, o_ref, s_ref, *, tb):
    @pl.when(pl.program_id(2) == 0)
    def _():
        s_ref[...] = jnp.zeros_like(s_ref)

    S = s_ref[...]
    for c in range(tb // GLA_CHUNK):
        sl = slice(c * GLA_CHUNK, (c + 1) * GLA_CHUNK)
        o, S = _gla_chunk(q_ref[sl, :].astype(F32), k_ref[sl, :].astype(F32), v_ref[sl, :],
                          z_ref[sl, :], S, reverse=False)
        o_ref[sl, :] = o
    s_ref[...] = S


def _gla_bwd_kernel(q_ref, k_ref, v_ref, z_ref, of_ref, r_ref, gain_ref, o_ref, s_ref, *, tb):
    @pl.when(pl.program_id(2) == 0)
    def _():
        s_ref[...] = jnp.zeros_like(s_ref)

    S = s_ref[...]
    for c in reversed(range(tb // GLA_CHUNK)):
        sl = slice(c * GLA_CHUNK, (c + 1) * GLA_CHUNK)
        o, S = _gla_chunk(q_ref[sl, :].astype(F32), k_ref[sl, :].astype(F32), v_ref[sl, :],
                          z_ref[sl, :], S, reverse=True)
        o = o + of_ref[sl, :]
        o = _rms_gain(o, gain_ref[...])
        o_ref[sl, :] = (o * _silu(r_ref[sl, :].astype(F32))).astype(o_ref.dtype)
    s_ref[...] = S


def _gla_core(main, z, gla_norm, B, L):
    T = B * L
    tb = min(GLA_TIME_BLOCK, L)
    nt = L // tb
    H = GLA_HEADS
    dk, dv = GLA_DK_HEAD, GLA_DV_HEAD
    grid = (B, H, nt)
    scratch = [pltpu.VMEM((dk, dv), F32)]

    def fwd_t(b, h, t):
        return b * nt + t

    def bwd_t(b, h, t):
        return b * nt + (nt - 1 - t)

    o_fwd = pl.pallas_call(
        functools.partial(_gla_fwd_kernel, tb=tb),
        grid=grid,
        in_specs=[pl.BlockSpec((tb, dk), lambda b, h, t: (fwd_t(b, h, t), h)),
                  pl.BlockSpec((tb, dk), lambda b, h, t: (fwd_t(b, h, t), H + h)),
                  pl.BlockSpec((tb, dv), lambda b, h, t: (fwd_t(b, h, t), H + h)),
                  pl.BlockSpec((tb, dk), lambda b, h, t: (fwd_t(b, h, t), h))],
        out_specs=pl.BlockSpec((tb, dv), lambda b, h, t: (fwd_t(b, h, t), h)),
        out_shape=jax.ShapeDtypeStruct((T, GLA_DV), F32),
        scratch_shapes=scratch,
        compiler_params=_cparams("parallel", "parallel", "arbitrary"),
        name="gla_forward",
    )(main, main, main, z)
    return pl.pallas_call(
        functools.partial(_gla_bwd_kernel, tb=tb),
        grid=grid,
        in_specs=[pl.BlockSpec((tb, dk), lambda b, h, t: (bwd_t(b, h, t), h)),
                  pl.BlockSpec((tb, dk), lambda b, h, t: (bwd_t(b, h, t), H + h)),
                  pl.BlockSpec((tb, dv), lambda b, h, t: (bwd_t(b, h, t), H + h)),
                  pl.BlockSpec((tb, dk), lambda b, h, t: (bwd_t(b, h, t), H + h)),
                  pl.BlockSpec((tb, dv), lambda b, h, t: (bwd_t(b, h, t), h)),
                  pl.BlockSpec((tb, dv), lambda b, h, t: (bwd_t(b, h, t), 2 * H + h)),
                  pl.BlockSpec((1, dv), lambda b, h, t: (0, h))],
        out_specs=pl.BlockSpec((tb, dv), lambda b, h, t: (bwd_t(b, h, t), h)),
        out_shape=jax.ShapeDtypeStruct((T, GLA_DV), BF16),
        scratch_shapes=scratch,
        compiler_params=_cparams("parallel", "parallel", "arbitrary"),
        name="gla_backward_finish",
    )(main, main, main, z, o_fwd, main, gla_norm)


def _na_bias_table(rpb):
    cols = jnp.arange(GRID_W)
    col_start = jnp.clip(cols - WIN_COLS // 2, 0, GRID_W - WIN_COLS)
    in_win = (cols[None, :] >= col_start[:, None]) & (cols[None, :] < col_start[:, None] + WIN_COLS)
    dc_idx = jnp.clip(cols[None, :] - cols[:, None] + WIN_COLS - 1, 0, 2 * WIN_COLS - 2)
    a = jnp.arange(NA_Q_ROWS)
    j = jnp.arange(NA_K_ROWS)
    tables = []
    for delta, win_start in ((0, jnp.zeros_like(a)), (4, a), (8, jnp.full_like(a, 4))):
        live_row = (j[None, :] >= win_start[:, None]) & (j[None, :] < win_start[:, None] + WIN_ROWS)
        dr_idx = jnp.clip(j[None, :] - delta - a[:, None] + WIN_ROWS - 1, 0, 2 * WIN_ROWS - 2)
        bias = rpb.astype(F32)[:, dr_idx][:, :, :, dc_idx]
        bias = jnp.transpose(bias, (0, 1, 3, 2, 4))
        live = live_row[:, None, :, None] & in_win[None, :, None, :]
        tables.append(jnp.where(live[None], bias, NEG_INF))
    t = jnp.stack(tables)
    return t.reshape(3, NA_HEADS, NA_Q_ROWS * GRID_W, NA_K_ROWS * GRID_W)


def _na_kernel(q_ref, k_ref, v_ref, bias_ref, o_ref, *, rows):
    ntile = rows // NA_Q_ROWS
    tq = NA_Q_ROWS * GRID_W
    tk = NA_K_ROWS * GRID_W
    lane = lax.broadcasted_iota(jnp.int32, (1, LANES), 1)
    first_head = lane < NA_HEAD_DIM

    def body(t, carry):
        kr0 = jnp.clip(NA_Q_ROWS * t - 4, 0, rows - NA_K_ROWS)
        geo = jnp.where(t == 0, 0, jnp.where(t == ntile - 1, 2, 1))
        q0 = pl.multiple_of(t * tq, tq)
        k0 = pl.multiple_of(kr0 * GRID_W, GRID_W)
        q = q_ref[pl.ds(q0, tq), :]
        k = k_ref[pl.ds(k0, tk), :]
        v = v_ref[pl.ds(k0, tk), :]
        outs = []
        for hh in range(2):
            sel = first_head if hh == 0 else jnp.logical_not(first_head)
            qh = jnp.where(sel, q, jnp.zeros_like(q))
            s = _dot_nt(qh, k) + bias_ref[geo, hh]
            m = jnp.max(s, axis=-1, keepdims=True)
            p = jnp.exp(s - m)
            l = jnp.sum(p, axis=-1, keepdims=True)
            outs.append(_dot(p.astype(BF16), v) * (1.0 / l))
        o_ref[pl.ds(q0, tq), :] = jnp.where(first_head, outs[0], outs[1]).astype(o_ref.dtype)
        return carry

    lax.fori_loop(0, ntile, body, 0)


def _na_core(qkv, bias, B, L):
    T = B * L
    rows = L // GRID_W
    npair = NA_HEADS // 2
    tq = NA_Q_ROWS * GRID_W
    tk = NA_K_ROWS * GRID_W
    return pl.pallas_call(
        functools.partial(_na_kernel, rows=rows),
        grid=(B, npair),
        in_specs=[pl.BlockSpec((L, LANES), lambda b, p: (b, p)),
                  pl.BlockSpec((L, LANES), lambda b, p: (b, npair + p)),
                  pl.BlockSpec((L, LANES), lambda b, p: (b, 2 * npair + p)),
                  pl.BlockSpec((3, 2, tq, tk), lambda b, p: (0, p, 0, 0))],
        out_specs=pl.BlockSpec((L, LANES), lambda b, p: (b, p)),
        out_shape=jax.ShapeDtypeStruct((T, D_MODEL), BF16),
        compiler_params=_cparams("parallel", "parallel"),
        name="neighbourhood_attention",
    )(qkv, qkv, qkv, bias)


def _router_kernel(x_ref, g_ref, mod_ref, wr_ref, h_ref, route_ref):
    m = mod_ref[0]
    h = _norm_mod(x_ref[...], g_ref[...], m[4:5], m[3:4])
    h_ref[...] = h
    logits = jnp.dot(h, wr_ref[...], preferred_element_type=F32, precision=lax.Precision.HIGHEST)
    lane = lax.broadcasted_iota(jnp.int32, logits.shape, 1)
    valid = lane < N_EXPERTS
    l1 = jnp.where(valid, logits, -jnp.inf)
    m1 = jnp.max(l1, axis=-1, keepdims=True)
    i1 = jnp.min(jnp.where(l1 == m1, lane, LANES), axis=-1, keepdims=True)
    l2 = jnp.where(lane == i1, -jnp.inf, l1)
    m2 = jnp.max(l2, axis=-1, keepdims=True)
    i2 = jnp.min(jnp.where(l2 == m2, lane, LANES), axis=-1, keepdims=True)
    e = jnp.exp(m2 - m1)
    g1 = 1.0 / (1.0 + e)
    g2 = e * g1
    route = jnp.where(lane == 0, i1.astype(F32),
                      jnp.where(lane == 1, i2.astype(F32),
                                jnp.where(lane == 2, g1, jnp.where(lane == 3, g2, 0.0))))
    route_ref[...] = route


def _router(x, gain, mod, wr, L):
    T = x.shape[0]
    tm = min(ROW_TILE, L)
    spt = L // tm
    return pl.pallas_call(
        _router_kernel,
        grid=(T // tm,),
        in_specs=[pl.BlockSpec((tm, D_MODEL), lambda i: (i, 0)),
                  pl.BlockSpec((1, D_MODEL), lambda i: (0, 0)),
                  pl.BlockSpec((1, 6, D_MODEL), lambda i: (i // spt, 0, 0)),
                  pl.BlockSpec((D_MODEL, LANES), lambda i: (0, 0))],
        out_specs=[pl.BlockSpec((tm, D_MODEL), lambda i: (i, 0)),
                   pl.BlockSpec((tm, LANES), lambda i: (i, 0))],
        out_shape=[jax.ShapeDtypeStruct((T, D_MODEL), F32),
                   jax.ShapeDtypeStruct((T, LANES), F32)],
        compiler_params=_cparams("parallel"),
        name="moe_router",
    )(x, gain, mod, wr)


def _moe_kernel(be_ref, nused_ref, idx_hbm, h_hbm, wg_ref, wu_ref, wd_ref, out_hbm,
                idx_smem, xbuf, xb16, acc_ref, ybuf, idx_sem, gat_sem, sca_sem, *, bm):
    n_spare_start = out_hbm.shape[0] - bm
    b = pl.program_id(0)
    f = pl.program_id(1)
    nf = pl.num_programs(1)
    nused = nused_ref[0]
    slot = b % 2

    def idx_copy(blk, s):
        return pltpu.make_async_copy(idx_hbm.at[blk], idx_smem.at[s], idx_sem.at[s])

    def gather_row(i, s):
        tok = idx_smem[s, i]
        return pltpu.make_async_copy(h_hbm.at[pl.ds(tok, 1)], xbuf.at[s, pl.ds(i, 1)], gat_sem.at[s])

    def start_gather(s):
        def issue(i, c):
            gather_row(i, s).start()
            return c
        lax.fori_loop(0, bm, issue, 0)

    def wait_gather(s):
        pltpu.make_async_copy(h_hbm.at[pl.ds(0, bm)], xbuf.at[s], gat_sem.at[s]).wait()

    def scatter_row(i, s):
        dst = idx_smem[s, bm + i]
        return pltpu.make_async_copy(ybuf.at[pl.ds(i, 1)], out_hbm.at[pl.ds(dst, 1)], sca_sem)

    def wait_scatter():
        pltpu.make_async_copy(ybuf, out_hbm.at[pl.ds(0, bm)], sca_sem).wait()

    @pl.when(b < nused)
    def _():
        @pl.when(f == 0)
        def _():
            @pl.when(b == 0)
            def _():
                idx_copy(0, 0).start()
                idx_copy(0, 0).wait()
                start_gather(0)

            wait_gather(slot)
            xb16[...] = xbuf[slot].astype(BF16)

        @pl.when((f == 1) & (b + 1 < nused))
        def _():
            idx_copy(b + 1, 1 - slot).start()

        @pl.when((f == 2) & (b + 1 < nused))
        def _():
            idx_copy(b + 1, 1 - slot).wait()
            start_gather(1 - slot)

        x = xb16[...]
        a = (_silu(_dot(x, wg_ref[0])) * _dot(x, wu_ref[0])).astype(BF16)
        part = _dot(a, wd_ref[0])

        @pl.when(f == 0)
        def _():
            acc_ref[...] = part

        @pl.when(f > 0)
        def _():
            acc_ref[...] += part

        @pl.when(f == nf - 1)
        def _():
            @pl.when(b > 0)
            def _():
                wait_scatter()

            ybuf[...] = acc_ref[...]

            @pl.when(b == 0)
            def _():
                spare = pltpu.make_async_copy(ybuf, out_hbm.at[pl.ds(n_spare_start, bm)], sca_sem)
                spare.start()
                spare.wait()

            def issue(i, c):
                scatter_row(i, slot).start()
                return c
            lax.fori_loop(0, bm, issue, 0)

            @pl.when(b == nused - 1)
            def _():
                wait_scatter()


def _moe_experts(block_e, nused, idx, h, wg, wu, wd, n_rows_out):
    n_blocks = idx.shape[0]
    bm = idx.shape[1] // 2
    tf = FFN_COL_TILE
    grid_spec = pltpu.PrefetchScalarGridSpec(
        num_scalar_prefetch=2,
        grid=(n_blocks, D_FF // tf),
        in_specs=[pl.BlockSpec(memory_space=pl.ANY),
                  pl.BlockSpec(memory_space=pl.ANY),
                  pl.BlockSpec((1, D_MODEL, tf), lambda b, f, be, nu: (be[b], 0, f)),
                  pl.BlockSpec((1, D_MODEL, tf), lambda b, f, be, nu: (be[b], 0, f)),
                  pl.BlockSpec((1, tf, D_MODEL), lambda b, f, be, nu: (be[b], f, 0))],
        out_specs=pl.BlockSpec(memory_space=pl.ANY),
        scratch_shapes=[pltpu.SMEM((2, 2 * bm), jnp.int32),
                        pltpu.VMEM((2, bm, D_MODEL), F32),
                        pltpu.VMEM((bm, D_MODEL), BF16),
                        pltpu.VMEM((bm, D_MODEL), F32),
                        pltpu.VMEM((bm, D_MODEL), F32),
                        pltpu.SemaphoreType.DMA((2,)),
                        pltpu.SemaphoreType.DMA((2,)),
                        pltpu.SemaphoreType.DMA(())],
    )
    return pl.pallas_call(
        functools.partial(_moe_kernel, bm=bm),
        grid_spec=grid_spec,
        out_shape=jax.ShapeDtypeStruct((n_rows_out, D_MODEL), F32),
        compiler_params=_cparams("arbitrary", "arbitrary"),
        name="moe_experts",
    )(block_e, nused, idx, h, wg, wu, wd)


def _combine_kernel(y2_ref, route_ref, x_ref, g_ref, mod_ref, o_ref):
    r = route_ref[...]
    y = r[:, 2:3] * y2_ref[:, :D_MODEL] + r[:, 3:4] * y2_ref[:, D_MODEL:]
    gate = mod_ref[0][5:6]
    o_ref[...] = x_ref[...] + gate * _rms_gain(y, g_ref[...])


def _moe_combine(y2, route, x, gain, mod, L):
    T = x.shape[0]
    tm = min(ROW_TILE, L)
    spt = L // tm
    return pl.pallas_call(
        _combine_kernel,
        grid=(T // tm,),
        in_specs=[pl.BlockSpec((tm, 2 * D_MODEL), lambda i: (i, 0)),
                  pl.BlockSpec((tm, LANES), lambda i: (i, 0)),
                  pl.BlockSpec((tm, D_MODEL), lambda i: (i, 0)),
                  pl.BlockSpec((1, D_MODEL), lambda i: (0, 0)),
                  pl.BlockSpec((1, 6, D_MODEL), lambda i: (i // spt, 0, 0))],
        out_specs=pl.BlockSpec((tm, D_MODEL), lambda i: (i, 0)),
        out_shape=jax.ShapeDtypeStruct((T, D_MODEL), F32),
        compiler_params=_cparams("parallel"),
        name="moe_combine_residual",
    )(y2, route, x, gain, mod)


def _moe_plan(route, T, bm):
    n_assign = TOP_K * T
    flat_e = route[:, :TOP_K].astype(jnp.int32).reshape(-1)
    order = jnp.argsort(flat_e, stable=True).astype(jnp.int32)
    counts = jnp.sum(flat_e[:, None] == jnp.arange(N_EXPERTS)[None, :], axis=0).astype(jnp.int32)
    padded = ((counts + bm - 1) // bm) * bm
    start = jnp.cumsum(counts) - counts
    pend = jnp.cumsum(padded)
    pstart = pend - padded
    n_blocks = n_assign // bm + N_EXPERTS
    P = n_blocks * bm
    pos = jnp.arange(P, dtype=jnp.int32)
    block_e = jnp.minimum(jnp.searchsorted(pend, jnp.arange(n_blocks, dtype=jnp.int32) * bm,
                                           side='right'), N_EXPERTS - 1).astype(jnp.int32)
    pe = jnp.repeat(block_e, bm)
    rank = pos - pstart[pe]
    valid = (rank < counts[pe]) & (pos < pend[N_EXPERTS - 1])
    a = order[jnp.clip(start[pe] + rank, 0, n_assign - 1)]
    src = jnp.where(valid, a // TOP_K, 0)
    dst = jnp.where(valid, a, n_assign + pos % bm)
    idx = jnp.concatenate([src.reshape(n_blocks, bm), dst.reshape(n_blocks, bm)], axis=1)
    nused = (pend[N_EXPERTS - 1] // bm).astype(jnp.int32).reshape(1)
    return block_e, nused, idx


def _moe_layer(x, gain2, gain3, mod, wr, wg, wu, wd, L):
    T = x.shape[0]
    bm = min(MOE_BLOCK, T)
    h, route = _router(x, gain2, mod, wr, L)
    block_e, nused, idx = _moe_plan(route, T, bm)
    y2 = _moe_experts(block_e, nused, idx, h, wg, wu, wd, TOP_K * T + bm)
    y2 = y2.reshape((TOP_K * T + bm) // 2, 2 * D_MODEL)
    return _moe_combine(y2, route, x, gain3, mod, L)


def _prep_weights(gla_w_in, gla_w_gate_up, gla_b_gate, gla_norm, gla_w_out, na_w_qkv, na_rpb,
                  na_w_out, ffn_w_gate, ffn_w_up, ffn_w_down, moe_router, moe_w_gate, moe_w_up,
                  moe_w_down):
    n_a = gla_w_in.shape[0]
    n_b = na_w_qkv.shape[0]
    R = GLA_GATE_RANK
    n_qkvr = 2 * GLA_DK + 2 * GLA_DV
    prep = {'gla': [], 'na': [], 'ffn': [], 'moe': []}
    for j in range(n_a):
        w = gla_w_in[j]
        w1 = jnp.concatenate([w[:, n_qkvr:n_qkvr + R], w[:, n_qkvr + R:]], axis=0)
        w1 = jnp.pad(w1, ((0, 0), (0, LANES - R)))
        w2 = jnp.pad(gla_w_gate_up[j], ((0, 0), (0, LANES - R), (0, 0)))
        fold = _small_matmul(w1, w2, jnp.zeros((2, 1, GLA_DK), F32),
                             act=False, col_tile=GLA_DK)
        w_z = jnp.concatenate([fold[0, :D_MODEL], fold[1, D_MODEL:]], axis=1)
        w_q = w[:, :GLA_DK] * (GLA_DK_HEAD ** -0.5)
        w_all = jnp.concatenate([w_q, w[:, GLA_DK:n_qkvr], w_z], axis=1).astype(BF16)
        prep['gla'].append(dict(w_all=w_all, zb=gla_b_gate[j].reshape(1, 2 * GLA_DK),
                                norm=gla_norm[j].reshape(1, GLA_DV),
                                w_out=gla_w_out[j].astype(BF16)))
        prep['ffn'].append(dict(wg=ffn_w_gate[j].astype(BF16), wu=ffn_w_up[j].astype(BF16),
                                wd=ffn_w_down[j].astype(BF16)))
    for j in range(n_b):
        w = na_w_qkv[j]
        w_qkv = jnp.concatenate([w[:, :D_MODEL] * (NA_HEAD_DIM ** -0.5), w[:, D_MODEL:]],
                                axis=1).astype(BF16)
        prep['na'].append(dict(w_qkv=w_qkv, bias=_na_bias_table(na_rpb[j]),
                               w_out=na_w_out[j].astype(BF16)))
        wr = jnp.zeros((D_MODEL, LANES), F32).at[:, :N_EXPERTS].set(moe_router[j])
        prep['moe'].append(dict(wr=wr, wg=moe_w_gate[j].astype(BF16), wu=moe_w_up[j].astype(BF16),
                                wd=moe_w_down[j].astype(BF16)))
    return prep


def _trunk(x, mod_all, norm_gains, prep):
    B, L, _ = x.shape
    x = x.reshape(B * L, D_MODEL)
    for i in range(DEPTH):
        j = i // 2
        mod = mod_all[i]
        gains = [norm_gains[i, n].reshape(1, D_MODEL) for n in range(4)]
        if i % 2 == 0:
            p = prep['gla'][j]
            main, z = _proj(x, gains[0], mod, p['w_all'], p['zb'], L,
                            2 * GLA_DK + 2 * GLA_DV, 2 * GLA_DK)
            o = _gla_core(main, z, p['norm'], B, L)
            x = _outproj(o, p['w_out'], x, gains[1], mod, L)
            f = prep['ffn'][j]
            x = _ffn(x, gains[2], gains[3], mod, f['wg'], f['wu'], f['wd'], L)
        else:
            p = prep['na'][j]
            qkv = _proj(x, gains[0], mod, p['w_qkv'], None, L, 3 * D_MODEL, 0)
            o = _na_core(qkv, p['bias'], B, L)
            x = _outproj(o, p['w_out'], x, gains[1], mod, L)
            m = prep['moe'][j]
            x = _moe_layer(x, gains[2], gains[3], mod, m['wr'], m['wg'], m['wu'], m['wd'], L)
    return x.reshape(B, L, D_MODEL)


def kernel(x_prompt, x_sample, c_prompt, c_sample, ada_w, ada_b, norm_gains, gla_w_in, gla_w_gate_up, gla_b_gate, gla_norm, gla_w_out, na_w_qkv, na_rpb, na_w_out, ffn_w_gate, ffn_w_up, ffn_w_down, moe_router, moe_w_gate, moe_w_up, moe_w_down):
    prep = _prep_weights(gla_w_in, gla_w_gate_up, gla_b_gate, gla_norm, gla_w_out, na_w_qkv, na_rpb,
                         na_w_out, ffn_w_gate, ffn_w_up, ffn_w_down, moe_router, moe_w_gate,
                         moe_w_up, moe_w_down)
    bp, bs = c_prompt.shape[0], c_sample.shape[0]
    rows = -(-(bp + bs) // 8) * 8
    c = jnp.zeros((rows, D_MODEL), F32).at[:bp].set(c_prompt).at[bp:bp + bs].set(c_sample)
    mod = _small_matmul(c, ada_w, ada_b.reshape(DEPTH, 1, 6 * D_MODEL), act=True, col_tile=D_MODEL)
    mod_p = mod[:, :bp].reshape(DEPTH, bp, 6, D_MODEL)
    mod_s = mod[:, bp:bp + bs].reshape(DEPTH, bs, 6, D_MODEL)
    y_prompt = _trunk(x_prompt, mod_p, norm_gains, prep)
    y_sample = _trunk(x_sample, mod_s, norm_gains, prep)
    return (y_prompt, y_sample)
```

```python
import functools

import jax
import jax.numpy as jnp
from jax import lax
from jax.experimental import pallas as pl
from jax.experimental.pallas import tpu as pltpu

D_MODEL = 1024
DEPTH = 4
GRID_W = 64
GLA_HEADS = 4
GLA_DK = D_MODEL // 2
GLA_DV = D_MODEL
GLA_DK_HEAD = GLA_DK // GLA_HEADS
GLA_DV_HEAD = GLA_DV // GLA_HEADS
GLA_GATE_RANK = 16
GLA_TAU = 16.0
GLA_CHUNK = 64
NA_HEADS = 16
NA_HEAD_DIM = D_MODEL // NA_HEADS
WIN_ROWS = 8
WIN_COLS = 16
D_FF = 7 * D_MODEL // 2
N_EXPERTS = 8
TOP_K = 2
NORM_EPS = 1e-6
NEG_INF = -1e30

BF16 = jnp.bfloat16
F32 = jnp.float32

VMEM_LIMIT_BYTES = 56 * 1024 * 1024
LANES = 128
SUBLANES = 8

ROW_TILE = 512
FFN_ROW_TILE = 1024
FFN_COL_TILE = 512
GLA_TIME_BLOCK = 512
NA_Q_ROWS = 4
NA_K_ROWS = 12
NA_TILES_PER_STEP = 2
MOE_BLOCK = 1024
MOE_ISSUE_STEPS = D_FF // FFN_COL_TILE


def _cparams(*sem):
    return pltpu.CompilerParams(dimension_semantics=sem, vmem_limit_bytes=VMEM_LIMIT_BYTES)


def _norm_mod(x, gain, scale, shift):
    ms = jnp.mean(x * x, axis=-1, keepdims=True)
    y = x * lax.rsqrt(ms + NORM_EPS) * gain
    return y * (1.0 + scale) + shift


def _rms_gain(y, gain):
    ms = jnp.mean(y * y, axis=-1, keepdims=True)
    return y * lax.rsqrt(ms + NORM_EPS) * gain


def _dot(a, b):
    return jnp.dot(a, b, preferred_element_type=F32)


def _dot_nt(a, b):
    return lax.dot_general(a, b, (((1,), (1,)), ((), ())), preferred_element_type=F32)


def _dot_tn(a, b):
    return lax.dot_general(a, b, (((0,), (0,)), ((), ())), preferred_element_type=F32)


def _silu(x):
    return x * (1.0 / (1.0 + jnp.exp(-x)))


def _small_matmul_kernel(a_ref, w_ref, b_ref, o_ref, *, act):
    a = a_ref[...]
    if act:
        a = _silu(a)
    o_ref[0] = jnp.dot(a, w_ref[0], preferred_element_type=F32,
                       precision=lax.Precision.HIGHEST) + b_ref[0]


def _small_matmul(a, w, b, *, act, col_tile):
    M, K = a.shape
    G, _, N = w.shape
    return pl.pallas_call(
        functools.partial(_small_matmul_kernel, act=act),
        grid=(G, N // col_tile),
        in_specs=[pl.BlockSpec((M, K), lambda g, j: (0, 0)),
                  pl.BlockSpec((1, K, col_tile), lambda g, j: (g, 0, j)),
                  pl.BlockSpec((1, 1, col_tile), lambda g, j: (g, 0, j))],
        out_specs=pl.BlockSpec((1, M, col_tile), lambda g, j: (g, 0, j)),
        out_shape=jax.ShapeDtypeStruct((G, M, N), F32),
        compiler_params=_cparams("parallel", "parallel"),
        name="small_matmul",
    )(a, w, b)


def _proj_kernel(x_ref, g_ref, mod_ref, w_ref, zb_ref, o_ref, z_ref, *, n_main, n_z, cn):
    m = mod_ref[0]
    h = _norm_mod(x_ref[...], g_ref[...], m[1:2], m[0:1]).astype(BF16)
    for n in range(0, n_main, cn):
        o_ref[:, n:n + cn] = _dot(h, w_ref[:, n:n + cn]).astype(o_ref.dtype)
    for n in range(0, n_z, cn):
        z_ref[:, n:n + cn] = _dot(h, w_ref[:, n_main + n:n_main + n + cn]) + zb_ref[:, n:n + cn]


def _proj_kernel_noz(x_ref, g_ref, mod_ref, w_ref, o_ref, *, n_main, cn):
    m = mod_ref[0]
    h = _norm_mod(x_ref[...], g_ref[...], m[1:2], m[0:1]).astype(BF16)
    for n in range(0, n_main, cn):
        o_ref[:, n:n + cn] = _dot(h, w_ref[:, n:n + cn]).astype(o_ref.dtype)


def _proj(x, gain, mod, w, zb, L, n_main, n_z):
    T = x.shape[0]
    tm = min(ROW_TILE, L)
    spt = L // tm
    N = n_main + n_z
    in_specs = [pl.BlockSpec((tm, D_MODEL), lambda i: (i, 0)),
                pl.BlockSpec((1, D_MODEL), lambda i: (0, 0)),
                pl.BlockSpec((1, 6, D_MODEL), lambda i: (i // spt, 0, 0)),
                pl.BlockSpec((D_MODEL, N), lambda i: (0, 0))]
    if n_z:
        return pl.pallas_call(
            functools.partial(_proj_kernel, n_main=n_main, n_z=n_z, cn=512),
            grid=(T // tm,),
            in_specs=in_specs + [pl.BlockSpec((1, n_z), lambda i: (0, 0))],
            out_specs=[pl.BlockSpec((tm, n_main), lambda i: (i, 0)),
                       pl.BlockSpec((tm, n_z), lambda i: (i, 0))],
            out_shape=[jax.ShapeDtypeStruct((T, n_main), BF16),
                       jax.ShapeDtypeStruct((T, n_z), F32)],
            compiler_params=_cparams("parallel"),
            name="norm_proj_gla",
        )(x, gain, mod, w, zb)
    return pl.pallas_call(
        functools.partial(_proj_kernel_noz, n_main=n_main, cn=512),
        grid=(T // tm,),
        in_specs=in_specs,
        out_specs=pl.BlockSpec((tm, n_main), lambda i: (i, 0)),
        out_shape=jax.ShapeDtypeStruct((T, n_main), BF16),
        compiler_params=_cparams("parallel"),
        name="norm_proj_na",
    )(x, gain, mod, w)


def _outproj_kernel(o_ref, w_ref, x_ref, g_ref, mod_ref, y_ref):
    y = _dot(o_ref[...], w_ref[...])
    gate = mod_ref[0][2:3]
    y_ref[...] = x_ref[...] + gate * _rms_gain(y, g_ref[...])


def _outproj(o, w, x, gain, mod, L):
    T, K = o.shape
    tm = min(ROW_TILE, L)
    spt = L // tm
    return pl.pallas_call(
        _outproj_kernel,
        grid=(T // tm,),
        in_specs=[pl.BlockSpec((tm, K), lambda i: (i, 0)),
                  pl.BlockSpec((K, D_MODEL), lambda i: (0, 0)),
                  pl.BlockSpec((tm, D_MODEL), lambda i: (i, 0)),
                  pl.BlockSpec((1, D_MODEL), lambda i: (0, 0)),
                  pl.BlockSpec((1, 6, D_MODEL), lambda i: (i // spt, 0, 0))],
        out_specs=pl.BlockSpec((tm, D_MODEL), lambda i: (i, 0)),
        out_shape=jax.ShapeDtypeStruct((T, D_MODEL), F32),
        compiler_params=_cparams("parallel"),
        name="outproj_residual",
    )(o, w, x, gain, mod)


def _ffn_kernel(x_ref, g2_ref, g3_ref, mod_ref, wg_ref, wu_ref, wd_ref, y_ref, h_ref, acc_ref):
    f = pl.program_id(1)
    m = mod_ref[0]

    @pl.when(f == 0)
    def _():
        h_ref[...] = _norm_mod(x_ref[...], g2_ref[...], m[4:5], m[3:4]).astype(BF16)

    h = h_ref[...]
    a = (_silu(_dot(h, wg_ref[...])) * _dot(h, wu_ref[...])).astype(BF16)
    part = _dot(a, wd_ref[...])

    @pl.when(f == 0)
    def _():
        acc_ref[...] = part

    @pl.when(f > 0)
    def _():
        acc_ref[...] += part

    @pl.when(f == pl.num_programs(1) - 1)
    def _():
        y_ref[...] = x_ref[...] + m[5:6] * _rms_gain(acc_ref[...], g3_ref[...])


def _ffn(x, gain2, gain3, mod, wg, wu, wd, L):
    T = x.shape[0]
    tm = min(FFN_ROW_TILE, L)
    spt = L // tm
    tf = FFN_COL_TILE
    return pl.pallas_call(
        _ffn_kernel,
        grid=(T // tm, D_FF // tf),
        in_specs=[pl.BlockSpec((tm, D_MODEL), lambda i, f: (i, 0)),
                  pl.BlockSpec((1, D_MODEL), lambda i, f: (0, 0)),
                  pl.BlockSpec((1, D_MODEL), lambda i, f: (0, 0)),
                  pl.BlockSpec((1, 6, D_MODEL), lambda i, f: (i // spt, 0, 0)),
                  pl.BlockSpec((D_MODEL, tf), lambda i, f: (0, f)),
                  pl.BlockSpec((D_MODEL, tf), lambda i, f: (0, f)),
                  pl.BlockSpec((tf, D_MODEL), lambda i, f: (f, 0))],
        out_specs=pl.BlockSpec((tm, D_MODEL), lambda i, f: (i, 0)),
        out_shape=jax.ShapeDtypeStruct((T, D_MODEL), F32),
        scratch_shapes=[pltpu.VMEM((tm, D_MODEL), BF16), pltpu.VMEM((tm, D_MODEL), F32)],
        compiler_params=_cparams("parallel", "arbitrary"),
        name="ffn_swiglu",
    )(x, gain2, gain3, mod, wg, wu, wd)


def _gla_block(q_ref, k_ref, v_ref, z_ref, S, *, tb, reverse):
    C = GLA_CHUNK
    chunks = list(range(tb // C))
    order = chunks[::-1] if reverse else chunks
    sl = {c: slice(c * C, (c + 1) * C) for c in chunks}
    row = lax.broadcasted_iota(jnp.int32, (C, C), 0)
    col = lax.broadcasted_iota(jnp.int32, (C, C), 1)
    if reverse:
        tri = (col >= row).astype(F32)
        allowed = col > row
        i_ref, i_last = C - 1 - C // 2, 0
    else:
        tri = (row >= col).astype(F32)
        allowed = row >= col
        i_ref, i_last = C // 2, C - 1
    dk = q_ref.shape[1]
    reps = v_ref.shape[1] // dk

    z = z_ref[...]
    g = (jnp.minimum(z, 0.0) - jnp.log(1.0 + jnp.exp(-jnp.abs(z)))) * (1.0 / GLA_TAU)
    G = {c: jnp.dot(tri, g[sl[c]], preferred_element_type=F32, precision=lax.Precision.HIGHEST)
         for c in chunks}
    q_rel, k_rel, q_in, k_out, decay = {}, {}, {}, {}, {}
    for c in chunks:
        q = q_ref[sl[c], :].astype(F32)
        k = k_ref[sl[c], :].astype(F32)
        G_ref = G[c][i_ref:i_ref + 1]
        G_last = G[c][i_last:i_last + 1]
        q_rel[c] = (q * jnp.exp(G[c] - G_ref)).astype(BF16)
        k_rel[c] = (k * jnp.exp(G_ref - G[c])).astype(BF16)
        q_in[c] = (q * jnp.exp(G[c])).astype(BF16)
        k_out[c] = (k * jnp.exp(G_last - G[c])).astype(BF16)
        d = jnp.broadcast_to(jnp.exp(G_last), (dk, dk)).T
        decay[c] = jnp.concatenate([d] * reps, axis=1)
    scores = {c: jnp.where(allowed, _dot_nt(q_rel[c], k_rel[c]), 0.0).astype(BF16) for c in chunks}
    kv = {c: _dot_tn(k_out[c], v_ref[sl[c], :]) for c in chunks}
    S_in = {}
    for c in order:
        S_in[c] = S.astype(BF16)
        S = decay[c] * S + kv[c]
    o = {c: _dot(scores[c], v_ref[sl[c], :]) + _dot(q_in[c], S_in[c]) for c in chunks}
    return o, S


def _gla_fwd_kernel(q_ref, k_ref, v_ref, z_ref, o_ref, s_ref, *, tb):
    @pl.when(pl.program_id(2) == 0)
    def _():
        s_ref[...] = jnp.zeros_like(s_ref)

    o, S = _gla_block(q_ref, k_ref, v_ref, z_ref, s_ref[...], tb=tb, reverse=False)
    for c, oc in o.items():
        o_ref[c * GLA_CHUNK:(c + 1) * GLA_CHUNK, :] = oc
    s_ref[...] = S


def _gla_bwd_kernel(q_ref, k_ref, v_ref, z_ref, of_ref, r_ref, gain_ref, o_ref, s_ref, *, tb):
    @pl.when(pl.program_id(2) == 0)
    def _():
        s_ref[...] = jnp.zeros_like(s_ref)

    o, S = _gla_block(q_ref, k_ref, v_ref, z_ref, s_ref[...], tb=tb, reverse=True)
    for c, oc in o.items():
        sl = slice(c * GLA_CHUNK, (c + 1) * GLA_CHUNK)
        oc = _rms_gain(oc + of_ref[sl, :], gain_ref[...])
        o_ref[sl, :] = (oc * _silu(r_ref[sl, :].astype(F32))).astype(o_ref.dtype)
    s_ref[...] = S


def _gla_core(main, z, gla_norm, B, L):
    T = B * L
    tb = min(GLA_TIME_BLOCK, L)
    nt = L // tb
    H = GLA_HEADS
    dk, dv = GLA_DK_HEAD, GLA_DV_HEAD
    grid = (B, H, nt)
    scratch = [pltpu.VMEM((dk, dv), F32)]

    def fwd_t(b, h, t):
        return b * nt + t

    def bwd_t(b, h, t):
        return b * nt + (nt - 1 - t)

    o_fwd = pl.pallas_call(
        functools.partial(_gla_fwd_kernel, tb=tb),
        grid=grid,
        in_specs=[pl.BlockSpec((tb, dk), lambda b, h, t: (fwd_t(b, h, t), h)),
                  pl.BlockSpec((tb, dk), lambda b, h, t: (fwd_t(b, h, t), H + h)),
                  pl.BlockSpec((tb, dv), lambda b, h, t: (fwd_t(b, h, t), H + h)),
                  pl.BlockSpec((tb, dk), lambda b, h, t: (fwd_t(b, h, t), h))],
        out_specs=pl.BlockSpec((tb, dv), lambda b, h, t: (fwd_t(b, h, t), h)),
        out_shape=jax.ShapeDtypeStruct((T, GLA_DV), F32),
        scratch_shapes=scratch,
        compiler_params=_cparams("parallel", "parallel", "arbitrary"),
        name="gla_forward",
    )(main, main, main, z)
    return pl.pallas_call(
        functools.partial(_gla_bwd_kernel, tb=tb),
        grid=grid,
        in_specs=[pl.BlockSpec((tb, dk), lambda b, h, t: (bwd_t(b, h, t), h)),
                  pl.BlockSpec((tb, dk), lambda b, h, t: (bwd_t(b, h, t), H + h)),
                  pl.BlockSpec((tb, dv), lambda b, h, t: (bwd_t(b, h, t), H + h)),
                  pl.BlockSpec((tb, dk), lambda b, h, t: (bwd_t(b, h, t), H + h)),
                  pl.BlockSpec((tb, dv), lambda b, h, t: (bwd_t(b, h, t), h)),
                  pl.BlockSpec((tb, dv), lambda b, h, t: (bwd_t(b, h, t), 2 * H + h)),
                  pl.BlockSpec((1, dv), lambda b, h, t: (0, h))],
        out_specs=pl.BlockSpec((tb, dv), lambda b, h, t: (bwd_t(b, h, t), h)),
        out_shape=jax.ShapeDtypeStruct((T, GLA_DV), BF16),
        scratch_shapes=scratch,
        compiler_params=_cparams("parallel", "parallel", "arbitrary"),
        name="gla_backward_finish",
    )(main, main, main, z, o_fwd, main, gla_norm)


def _na_bias_table(rpb):
    cols = jnp.arange(GRID_W)
    col_start = jnp.clip(cols - WIN_COLS // 2, 0, GRID_W - WIN_COLS)
    in_win = (cols[None, :] >= col_start[:, None]) & (cols[None, :] < col_start[:, None] + WIN_COLS)
    dc_idx = jnp.clip(cols[None, :] - cols[:, None] + WIN_COLS - 1, 0, 2 * WIN_COLS - 2)
    a = jnp.arange(NA_Q_ROWS)
    j = jnp.arange(NA_K_ROWS)
    tables = []
    for delta, win_start in ((0, jnp.zeros_like(a)), (4, a), (8, jnp.full_like(a, 4))):
        live_row = (j[None, :] >= win_start[:, None]) & (j[None, :] < win_start[:, None] + WIN_ROWS)
        dr_idx = jnp.clip(j[None, :] - delta - a[:, None] + WIN_ROWS - 1, 0, 2 * WIN_ROWS - 2)
        bias = rpb.astype(F32)[:, dr_idx][:, :, :, dc_idx]
        bias = jnp.transpose(bias, (0, 1, 3, 2, 4))
        live = live_row[:, None, :, None] & in_win[None, :, None, :]
        tables.append(jnp.where(live[None], bias, NEG_INF))
    t = jnp.stack(tables)
    return t.reshape(3, NA_HEADS, NA_Q_ROWS * GRID_W, NA_K_ROWS * GRID_W)


def _na_kernel(q_ref, k_ref, v_ref, bias_ref, o_ref, *, rows):
    ntile = rows // NA_Q_ROWS
    tq = NA_Q_ROWS * GRID_W
    tk = NA_K_ROWS * GRID_W
    lane = lax.broadcasted_iota(jnp.int32, (1, LANES), 1)
    first_head = lane < NA_HEAD_DIM

    def body(tt, carry):
        q0, geo, k, v, s = {}, {}, {}, {}, {}
        units = [(u, hh) for u in range(NA_TILES_PER_STEP) for hh in range(2)]
        for u in range(NA_TILES_PER_STEP):
            t = NA_TILES_PER_STEP * tt + u
            kr0 = jnp.clip(NA_Q_ROWS * t - 4, 0, rows - NA_K_ROWS)
            geo[u] = jnp.where(t == 0, 0, jnp.where(t == ntile - 1, 2, 1))
            q0[u] = pl.multiple_of(t * tq, tq)
            k0 = pl.multiple_of(kr0 * GRID_W, GRID_W)
            q = q_ref[pl.ds(q0[u], tq), :]
            k[u] = k_ref[pl.ds(k0, tk), :]
            v[u] = v_ref[pl.ds(k0, tk), :]
            for hh in range(2):
                sel = first_head if hh == 0 else jnp.logical_not(first_head)
                s[u, hh] = _dot_nt(jnp.where(sel, q, jnp.zeros_like(q)), k[u])
        p, inv_l, o = {}, {}, {}
        for u, hh in units:
            sb = s[u, hh] + bias_ref[geo[u], hh]
            e = jnp.exp(sb - jnp.max(sb, axis=-1, keepdims=True))
            inv_l[u, hh] = 1.0 / jnp.sum(e, axis=-1, keepdims=True)
            p[u, hh] = e.astype(BF16)
        for u, hh in units:
            o[u, hh] = _dot(p[u, hh], v[u]) * inv_l[u, hh]
        for u in range(NA_TILES_PER_STEP):
            o_ref[pl.ds(q0[u], tq), :] = jnp.where(first_head, o[u, 0], o[u, 1]).astype(o_ref.dtype)
        return carry

    lax.fori_loop(0, ntile // NA_TILES_PER_STEP, body, 0)


def _na_core(qkv, bias, B, L):
    T = B * L
    rows = L // GRID_W
    npair = NA_HEADS // 2
    tq = NA_Q_ROWS * GRID_W
    tk = NA_K_ROWS * GRID_W
    assert (rows // NA_Q_ROWS) % NA_TILES_PER_STEP == 0 and rows >= NA_K_ROWS
    return pl.pallas_call(
        functools.partial(_na_kernel, rows=rows),
        grid=(npair, B),
        in_specs=[pl.BlockSpec((L, LANES), lambda p, b: (b, p)),
                  pl.BlockSpec((L, LANES), lambda p, b: (b, npair + p)),
                  pl.BlockSpec((L, LANES), lambda p, b: (b, 2 * npair + p)),
                  pl.BlockSpec((3, 2, tq, tk), lambda p, b: (0, p, 0, 0))],
        out_specs=pl.BlockSpec((L, LANES), lambda p, b: (b, p)),
        out_shape=jax.ShapeDtypeStruct((T, D_MODEL), BF16),
        compiler_params=_cparams("parallel", "parallel"),
        name="neighbourhood_attention",
    )(qkv, qkv, qkv, bias)


def _router_kernel(x_ref, g_ref, mod_ref, wr_ref, h_ref, route_ref, cnt_ref, carry_ref):
    @pl.when(pl.program_id(0) == 0)
    def _():
        carry_ref[...] = jnp.zeros_like(carry_ref)

    m = mod_ref[0]
    h = _norm_mod(x_ref[...], g_ref[...], m[4:5], m[3:4])
    tm = h.shape[0]
    for s in range(SUBLANES):
        h_ref[pl.ds(s, tm, stride=SUBLANES), :] = h[:, s * LANES:(s + 1) * LANES]
    logits = jnp.dot(h, wr_ref[...], preferred_element_type=F32, precision=lax.Precision.HIGHEST)
    lane = lax.broadcasted_iota(jnp.int32, logits.shape, 1)
    valid = lane < N_EXPERTS
    l1 = jnp.where(valid, logits, -jnp.inf)
    m1 = jnp.max(l1, axis=-1, keepdims=True)
    i1 = jnp.min(jnp.where(l1 == m1, lane, LANES), axis=-1, keepdims=True)
    l2 = jnp.where(lane == i1, -jnp.inf, l1)
    m2 = jnp.max(l2, axis=-1, keepdims=True)
    i2 = jnp.min(jnp.where(l2 == m2, lane, LANES), axis=-1, keepdims=True)
    e = jnp.exp(m2 - m1)
    g1 = 1.0 / (1.0 + e)
    g2 = e * g1
    chosen = (lane == i1) | (lane == i2)
    row = lax.broadcasted_iota(jnp.int32, (tm, tm), 0)
    col = lax.broadcasted_iota(jnp.int32, (tm, tm), 1)
    earlier = jnp.where(row > col, 1.0, 0.0).astype(BF16)
    prefix = _dot(earlier, jnp.where(chosen, 1.0, 0.0).astype(BF16)) + carry_ref[0:1, :]
    rank1 = jnp.sum(jnp.where(lane == i1, prefix, 0.0), axis=-1, keepdims=True)
    rank2 = jnp.sum(jnp.where(lane == i2, prefix, 0.0), axis=-1, keepdims=True)
    carry_ref[...] = carry_ref[...] + jnp.sum(jnp.where(chosen, 1.0, 0.0), axis=0, keepdims=True)
    cnt_ref[...] = carry_ref[...]
    route = jnp.where(lane == 0, i1.astype(F32),
                      jnp.where(lane == 1, i2.astype(F32),
                                jnp.where(lane == 2, g1,
                                          jnp.where(lane == 3, g2,
                                                    jnp.where(lane == 4, rank1,
                                                              jnp.where(lane == 5, rank2, 0.0))))))
    route_ref[...] = route


def _router(x, gain, mod, wr, L):
    T = x.shape[0]
    tm = min(ROW_TILE, L)
    spt = L // tm
    return pl.pallas_call(
        _router_kernel,
        grid=(T // tm,),
        in_specs=[pl.BlockSpec((tm, D_MODEL), lambda i: (i, 0)),
                  pl.BlockSpec((1, D_MODEL), lambda i: (0, 0)),
                  pl.BlockSpec((1, 6, D_MODEL), lambda i: (i // spt, 0, 0)),
                  pl.BlockSpec((D_MODEL, LANES), lambda i: (0, 0))],
        out_specs=[pl.BlockSpec((tm * SUBLANES, LANES), lambda i: (i, 0)),
                   pl.BlockSpec((tm, LANES), lambda i: (i, 0)),
                   pl.BlockSpec((SUBLANES, LANES), lambda i: (0, 0))],
        out_shape=[jax.ShapeDtypeStruct((T * SUBLANES, LANES), F32),
                   jax.ShapeDtypeStruct((T, LANES), F32),
                   jax.ShapeDtypeStruct((SUBLANES, LANES), F32)],
        scratch_shapes=[pltpu.VMEM((SUBLANES, LANES), F32)],
        compiler_params=_cparams("arbitrary"),
        name="moe_router",
    )(x, gain, mod, wr)


def _moe_kernel(be_ref, srcx_hbm, dstx_hbm, h_hbm, wg_ref, wu_ref, wd_ref, out_hbm,
                src_smem, dst_smem, xbuf, xb16, acc_ref, ybuf,
                src_sem, dst_sem, gat_sem, sca_sem, *, bm):
    b = pl.program_id(0)
    f = pl.program_id(1)
    nb = pl.num_programs(0)
    nf = pl.num_programs(1)
    slot = b % 2
    nxt = 1 - slot
    group = -(-bm // MOE_ISSUE_STEPS)
    always = bm - (MOE_ISSUE_STEPS - 1) * group

    def src_copy(r, s):
        return pltpu.make_async_copy(srcx_hbm.at[r], src_smem.at[s], src_sem.at[s])

    def dst_copy(r, s):
        return pltpu.make_async_copy(dstx_hbm.at[r], dst_smem.at[s], dst_sem.at[s])

    def tile(ref, i):
        return ref.at[pl.ds(pl.multiple_of(i * SUBLANES, SUBLANES), SUBLANES)]

    def gather_row(i, s):
        return pltpu.make_async_copy(tile(h_hbm, src_smem[s, i]), tile(xbuf.at[s], i), gat_sem.at[s])

    def scatter_row(i, s):
        return pltpu.make_async_copy(tile(ybuf, i), tile(out_hbm, dst_smem[s, i]), sca_sem)

    def wait_gather(s):
        pltpu.make_async_copy(h_hbm.at[pl.ds(0, bm * SUBLANES)], xbuf.at[s], gat_sem.at[s]).wait()

    def wait_scatter():
        pltpu.make_async_copy(ybuf, out_hbm.at[pl.ds(0, bm * SUBLANES)], sca_sem).wait()

    def for_all_rows(start_row):
        def issue(i, c):
            start_row(i)
            return c
        lax.fori_loop(0, bm, issue, 0)

    @pl.when(f == 0)
    def _():
        @pl.when(b == 0)
        def _():
            src_copy(0, 0).start()
            src_copy(1, 1).start()
            dst_copy(0, 0).start()
            src_copy(0, 0).wait()
            for_all_rows(lambda i: gather_row(i, 0).start())
            ybuf[...] = jnp.zeros_like(ybuf)

        src_copy(b + 1, nxt).wait()
        dst_copy(b, slot).wait()
        wait_gather(slot)
        for s in range(SUBLANES):
            xb16[:, s * LANES:(s + 1) * LANES] = xbuf[slot, pl.ds(s, bm, stride=SUBLANES), :].astype(BF16)

    base = f * group
    for j in range(always):
        gather_row(base + j, nxt).start()
        scatter_row(base + j, slot).start()

    x = xb16[...]
    a = (_silu(_dot(x, wg_ref[0])) * _dot(x, wu_ref[0])).astype(BF16)
    part = _dot(a, wd_ref[0])

    @pl.when(f == 0)
    def _():
        acc_ref[...] = part

    @pl.when(f > 0)
    def _():
        acc_ref[...] += part

    @pl.when(f < nf - 1)
    def _():
        for j in range(always, group):
            gather_row(base + j, nxt).start()
            scatter_row(base + j, slot).start()

    @pl.when(f == nf - 1)
    def _():
        wait_scatter()
        for s in range(SUBLANES):
            ybuf[pl.ds(s, bm, stride=SUBLANES), :] = acc_ref[:, s * LANES:(s + 1) * LANES]
        dst_copy(b + 1, nxt).start()

        @pl.when(b + 2 <= nb)
        def _():
            src_copy(b + 2, slot).start()

        @pl.when(b == nb - 1)
        def _():
            dst_copy(b + 1, nxt).wait()
            for_all_rows(lambda i: scatter_row(i, nxt).start())
            wait_scatter()
            wait_gather(nxt)


def _moe_experts(block_e, srcx, dstx, h, wg, wu, wd, n_rows_out):
    n_blocks = srcx.shape[0] - 1
    bm = srcx.shape[1]
    tf = FFN_COL_TILE
    assert D_FF // tf == MOE_ISSUE_STEPS
    grid_spec = pltpu.PrefetchScalarGridSpec(
        num_scalar_prefetch=1,
        grid=(n_blocks, D_FF // tf),
        in_specs=[pl.BlockSpec(memory_space=pl.ANY),
                  pl.BlockSpec(memory_space=pl.ANY),
                  pl.BlockSpec(memory_space=pl.ANY),
                  pl.BlockSpec((1, D_MODEL, tf), lambda b, f, be: (be[b], 0, f)),
                  pl.BlockSpec((1, D_MODEL, tf), lambda b, f, be: (be[b], 0, f)),
                  pl.BlockSpec((1, tf, D_MODEL), lambda b, f, be: (be[b], f, 0))],
        out_specs=pl.BlockSpec(memory_space=pl.ANY),
        scratch_shapes=[pltpu.SMEM((2, bm), jnp.int32),
                        pltpu.SMEM((2, bm), jnp.int32),
                        pltpu.VMEM((2, bm * SUBLANES, LANES), F32),
                        pltpu.VMEM((bm, D_MODEL), BF16),
                        pltpu.VMEM((bm, D_MODEL), F32),
                        pltpu.VMEM((bm * SUBLANES, LANES), F32),
                        pltpu.SemaphoreType.DMA((2,)),
                        pltpu.SemaphoreType.DMA((2,)),
                        pltpu.SemaphoreType.DMA((2,)),
                        pltpu.SemaphoreType.DMA(())],
    )
    return pl.pallas_call(
        functools.partial(_moe_kernel, bm=bm),
        grid_spec=grid_spec,
        out_shape=jax.ShapeDtypeStruct((n_rows_out * SUBLANES, LANES), F32),
        compiler_params=_cparams("arbitrary", "arbitrary"),
        name="moe_experts",
    )(block_e, srcx, dstx, h, wg, wu, wd)


def _combine_kernel(y1_ref, y2_ref, route_ref, x_ref, g_ref, mod_ref, o_ref, y_ref):
    r = route_ref[...]
    tm = r.shape[0]
    for s in range(SUBLANES):
        rows = pl.ds(s, tm, stride=SUBLANES)
        y_ref[:, s * LANES:(s + 1) * LANES] = r[:, 2:3] * y1_ref[rows, :] + r[:, 3:4] * y2_ref[rows, :]
    gate = mod_ref[0][5:6]
    o_ref[...] = x_ref[...] + gate * _rms_gain(y_ref[...], g_ref[...])


def _moe_combine(y, route, x, gain, mod, L):
    T = x.shape[0]
    tm = min(ROW_TILE, L)
    spt = L // tm
    nt = T // tm
    return pl.pallas_call(
        _combine_kernel,
        grid=(nt,),
        in_specs=[pl.BlockSpec((tm * SUBLANES, LANES), lambda i: (i, 0)),
                  pl.BlockSpec((tm * SUBLANES, LANES), lambda i: (nt + i, 0)),
                  pl.BlockSpec((tm, LANES), lambda i: (i, 0)),
                  pl.BlockSpec((tm, D_MODEL), lambda i: (i, 0)),
                  pl.BlockSpec((1, D_MODEL), lambda i: (0, 0)),
                  pl.BlockSpec((1, 6, D_MODEL), lambda i: (i // spt, 0, 0))],
        out_specs=pl.BlockSpec((tm, D_MODEL), lambda i: (i, 0)),
        out_shape=jax.ShapeDtypeStruct((T, D_MODEL), F32),
        scratch_shapes=[pltpu.VMEM((tm, D_MODEL), F32)],
        compiler_params=_cparams("parallel"),
        name="moe_combine_residual",
    )(y, y, route, x, gain, mod)


def _moe_plan(route, counts, T, bm):
    n_assign = TOP_K * T
    n_blocks = n_assign // bm + N_EXPERTS
    e = route[:, 0:TOP_K].astype(jnp.int32)
    rank = route[:, 4:4 + TOP_K].astype(jnp.int32)
    counts = counts[0, :N_EXPERTS].astype(jnp.int32)
    padded = ((counts + bm - 1) // bm) * bm
    pend = jnp.cumsum(padded)
    pstart = pend - padded
    dest = jnp.take(pstart, e) + rank
    out_row = jnp.arange(T, dtype=jnp.int32)[:, None] + T * jnp.arange(TOP_K, dtype=jnp.int32)[None, :]
    pos = jnp.arange(n_blocks * bm, dtype=jnp.int32)
    spare = n_assign + pos % bm
    dst = spare.at[dest.reshape(-1)].set(out_row.reshape(-1), unique_indices=True)
    src = jnp.where(dst < n_assign, dst % T, 0)
    block_start = jnp.arange(n_blocks, dtype=jnp.int32) * bm
    block_e = jnp.minimum(jnp.sum(block_start[:, None] >= pend[None, :], axis=1),
                          N_EXPERTS - 1).astype(jnp.int32)
    srcx = jnp.concatenate([src.reshape(n_blocks, bm), jnp.zeros((1, bm), jnp.int32)], axis=0)
    dstx = jnp.concatenate([spare[:bm].reshape(1, bm), dst.reshape(n_blocks, bm)], axis=0)
    return block_e, srcx, dstx


def _moe_layer(x, gain2, gain3, mod, wr, wg, wu, wd, L):
    T = x.shape[0]
    bm = min(MOE_BLOCK, T)
    h, route, counts = _router(x, gain2, mod, wr, L)
    block_e, srcx, dstx = _moe_plan(route, counts, T, bm)
    y = _moe_experts(block_e, srcx, dstx, h, wg, wu, wd, TOP_K * T + bm)
    return _moe_combine(y, route, x, gain3, mod, L)


def _prep_weights(gla_w_in, gla_w_gate_up, gla_b_gate, gla_norm, gla_w_out, na_w_qkv, na_rpb,
                  na_w_out, ffn_w_gate, ffn_w_up, ffn_w_down, moe_router, moe_w_gate, moe_w_up,
                  moe_w_down):
    n_a = gla_w_in.shape[0]
    n_b = na_w_qkv.shape[0]
    R = GLA_GATE_RANK
    n_qkvr = 2 * GLA_DK + 2 * GLA_DV
    prep = {'gla': [], 'na': [], 'ffn': [], 'moe': []}
    for j in range(n_a):
        w = gla_w_in[j]
        w1 = jnp.concatenate([w[:, n_qkvr:n_qkvr + R], w[:, n_qkvr + R:]], axis=0)
        w1 = jnp.pad(w1, ((0, 0), (0, LANES - R)))
        w2 = jnp.pad(gla_w_gate_up[j], ((0, 0), (0, LANES - R), (0, 0)))
        fold = _small_matmul(w1, w2, jnp.zeros((2, 1, GLA_DK), F32),
                             act=False, col_tile=GLA_DK)
        w_z = jnp.concatenate([fold[0, :D_MODEL], fold[1, D_MODEL:]], axis=1)
        w_q = w[:, :GLA_DK] * (GLA_DK_HEAD ** -0.5)
        w_all = jnp.concatenate([w_q, w[:, GLA_DK:n_qkvr], w_z], axis=1).astype(BF16)
        prep['gla'].append(dict(w_all=w_all, zb=gla_b_gate[j].reshape(1, 2 * GLA_DK),
                                norm=gla_norm[j].reshape(1, GLA_DV),
                                w_out=gla_w_out[j].astype(BF16)))
        prep['ffn'].append(dict(wg=ffn_w_gate[j].astype(BF16), wu=ffn_w_up[j].astype(BF16),
                                wd=ffn_w_down[j].astype(BF16)))
    for j in range(n_b):
        w = na_w_qkv[j]
        w_qkv = jnp.concatenate([w[:, :D_MODEL] * (NA_HEAD_DIM ** -0.5), w[:, D_MODEL:]],
                                axis=1).astype(BF16)
        prep['na'].append(dict(w_qkv=w_qkv, bias=_na_bias_table(na_rpb[j]),
                               w_out=na_w_out[j].astype(BF16)))
        wr = jnp.zeros((D_MODEL, LANES), F32).at[:, :N_EXPERTS].set(moe_router[j])
        prep['moe'].append(dict(wr=wr, wg=moe_w_gate[j].astype(BF16), wu=moe_w_up[j].astype(BF16),
                                wd=moe_w_down[j].astype(BF16)))
    return prep


def _trunk(x, mod_all, norm_gains, prep):
    B, L, _ = x.shape
    x = x.reshape(B * L, D_MODEL)
    for i in range(DEPTH):
        j = i // 2
        mod = mod_all[i]
        gains = [norm_gains[i, n].reshape(1, D_MODEL) for n in range(4)]
        if i % 2 == 0:
            p = prep['gla'][j]
            main, z = _proj(x, gains[0], mod, p['w_all'], p['zb'], L,
                            2 * GLA_DK + 2 * GLA_DV, 2 * GLA_DK)
            o = _gla_core(main, z, p['norm'], B, L)
            x = _outproj(o, p['w_out'], x, gains[1], mod, L)
            f = prep['ffn'][j]
            x = _ffn(x, gains[2], gains[3], mod, f['wg'], f['wu'], f['wd'], L)
        else:
            p = prep['na'][j]
            qkv = _proj(x, gains[0], mod, p['w_qkv'], None, L, 3 * D_MODEL, 0)
            o = _na_core(qkv, p['bias'], B, L)
            x = _outproj(o, p['w_out'], x, gains[1], mod, L)
            m = prep['moe'][j]
            x = _moe_layer(x, gains[2], gains[3], mod, m['wr'], m['wg'], m['wu'], m['wd'], L)
    return x.reshape(B, L, D_MODEL)


def kernel(x_prompt, x_sample, c_prompt, c_sample, ada_w, ada_b, norm_gains, gla_w_in, gla_w_gate_up, gla_b_gate, gla_norm, gla_w_out, na_w_qkv, na_rpb, na_w_out, ffn_w_gate, ffn_w_up, ffn_w_down, moe_router, moe_w_gate, moe_w_up, moe_w_down):
    prep = _prep_weights(gla_w_in, gla_w_gate_up, gla_b_gate, gla_norm, gla_w_out, na_w_qkv, na_rpb,
                         na_w_out, ffn_w_gate, ffn_w_up, ffn_w_down, moe_router, moe_w_gate,
                         moe_w_up, moe_w_down)
    bp, bs = c_prompt.shape[0], c_sample.shape[0]
    rows = -(-(bp + bs) // 8) * 8
    c = jnp.zeros((rows, D_MODEL), F32).at[:bp].set(c_prompt).at[bp:bp + bs].set(c_sample)
    mod = _small_matmul(c, ada_w, ada_b.reshape(DEPTH, 1, 6 * D_MODEL), act=True, col_tile=D_MODEL)
    mod_p = mod[:, :bp].reshape(DEPTH, bp, 6, D_MODEL)
    mod_s = mod[:, bp:bp + bs].reshape(DEPTH, bs, 6, D_MODEL)
    y_prompt = _trunk(x_prompt, mod_p, norm_gains, prep)
    y_sample = _trunk(x_sample, mod_s, norm_gains, prep)
    return (y_prompt, y_sample)
```

```python
import functools

import jax
import jax.numpy as jnp
from jax import lax
from jax.experimental import pallas as pl
from jax.experimental.pallas import tpu as pltpu

D_MODEL = 1024
DEPTH = 4
GRID_W = 64
GLA_HEADS = 4
GLA_DK = D_MODEL // 2
GLA_DV = D_MODEL
GLA_DK_HEAD = GLA_DK // GLA_HEADS
GLA_DV_HEAD = GLA_DV // GLA_HEADS
GLA_GATE_RANK = 16
GLA_TAU = 16.0
GLA_CHUNK = 64
NA_HEADS = 16
NA_HEAD_DIM = D_MODEL // NA_HEADS
WIN_ROWS = 8
WIN_COLS = 16
D_FF = 7 * D_MODEL // 2
N_EXPERTS = 8
TOP_K = 2
NORM_EPS = 1e-6
NEG_INF = -1e30
LOG2E = 1.4426950408889634

BF16 = jnp.bfloat16
F32 = jnp.float32

VMEM_LIMIT_BYTES = 56 * 1024 * 1024
LANES = 128
SUBLANES = 8

ROW_TILE = 512
FFN_ROW_TILE = 1024
FFN_COL_TILE = 512
GLA_TIME_BLOCK = 1024
NA_Q_ROWS = 4
NA_K_ROWS = 12
NA_TILES_PER_STEP = 2
MOE_BLOCK = 1024
MOE_ISSUE_STEPS = D_FF // FFN_COL_TILE


def _cparams(*sem):
    return pltpu.CompilerParams(dimension_semantics=sem, vmem_limit_bytes=VMEM_LIMIT_BYTES)


def _norm_mod(x, gain, scale, shift):
    ms = jnp.mean(x * x, axis=-1, keepdims=True)
    y = x * lax.rsqrt(ms + NORM_EPS) * gain
    return y * (1.0 + scale) + shift


def _rms_gain(y, gain):
    ms = jnp.mean(y * y, axis=-1, keepdims=True)
    return y * lax.rsqrt(ms + NORM_EPS) * gain


def _dot(a, b):
    return jnp.dot(a, b, preferred_element_type=F32)


def _dot_nt(a, b):
    return lax.dot_general(a, b, (((1,), (1,)), ((), ())), preferred_element_type=F32)


def _dot_tn(a, b):
    return lax.dot_general(a, b, (((0,), (0,)), ((), ())), preferred_element_type=F32)


def _silu(x):
    return x * (1.0 / (1.0 + jnp.exp(-x)))


def _small_matmul_kernel(a_ref, w_ref, b_ref, o_ref, *, act):
    a = a_ref[...]
    if act:
        a = _silu(a)
    o_ref[0] = jnp.dot(a, w_ref[0], preferred_element_type=F32,
                       precision=lax.Precision.HIGHEST) + b_ref[0]


def _small_matmul(a, w, b, *, act, col_tile):
    M, K = a.shape
    G, _, N = w.shape
    return pl.pallas_call(
        functools.partial(_small_matmul_kernel, act=act),
        grid=(G, N // col_tile),
        in_specs=[pl.BlockSpec((M, K), lambda g, j: (0, 0)),
                  pl.BlockSpec((1, K, col_tile), lambda g, j: (g, 0, j)),
                  pl.BlockSpec((1, 1, col_tile), lambda g, j: (g, 0, j))],
        out_specs=pl.BlockSpec((1, M, col_tile), lambda g, j: (g, 0, j)),
        out_shape=jax.ShapeDtypeStruct((G, M, N), F32),
        compiler_params=_cparams("parallel", "parallel"),
        name="small_matmul",
    )(a, w, b)


def _proj_kernel(x_ref, g_ref, mod_ref, w_ref, zb_ref, o_ref, z_ref, *, n_main, n_z, cn):
    m = mod_ref[0]
    h = _norm_mod(x_ref[...], g_ref[...], m[1:2], m[0:1]).astype(BF16)
    for n in range(0, n_main, cn):
        o_ref[:, n:n + cn] = _dot(h, w_ref[:, n:n + cn]).astype(o_ref.dtype)
    for n in range(0, n_z, cn):
        z_ref[:, n:n + cn] = _dot(h, w_ref[:, n_main + n:n_main + n + cn]) + zb_ref[:, n:n + cn]


def _proj_kernel_noz(x_ref, g_ref, mod_ref, w_ref, o_ref, *, n_main, cn):
    m = mod_ref[0]
    h = _norm_mod(x_ref[...], g_ref[...], m[1:2], m[0:1]).astype(BF16)
    for n in range(0, n_main, cn):
        o_ref[:, n:n + cn] = _dot(h, w_ref[:, n:n + cn]).astype(o_ref.dtype)


def _proj(x, gain, mod, w, zb, L, n_main, n_z):
    T = x.shape[0]
    tm = min(ROW_TILE, L)
    spt = L // tm
    N = n_main + n_z
    in_specs = [pl.BlockSpec((tm, D_MODEL), lambda i: (i, 0)),
                pl.BlockSpec((1, D_MODEL), lambda i: (0, 0)),
                pl.BlockSpec((1, 6, D_MODEL), lambda i: (i // spt, 0, 0)),
                pl.BlockSpec((D_MODEL, N), lambda i: (0, 0))]
    if n_z:
        return pl.pallas_call(
            functools.partial(_proj_kernel, n_main=n_main, n_z=n_z, cn=512),
            grid=(T // tm,),
            in_specs=in_specs + [pl.BlockSpec((1, n_z), lambda i: (0, 0))],
            out_specs=[pl.BlockSpec((tm, n_main), lambda i: (i, 0)),
                       pl.BlockSpec((tm, n_z), lambda i: (i, 0))],
            out_shape=[jax.ShapeDtypeStruct((T, n_main), BF16),
                       jax.ShapeDtypeStruct((T, n_z), F32)],
            compiler_params=_cparams("parallel"),
            name="norm_proj_gla",
        )(x, gain, mod, w, zb)
    return pl.pallas_call(
        functools.partial(_proj_kernel_noz, n_main=n_main, cn=512),
        grid=(T // tm,),
        in_specs=in_specs,
        out_specs=pl.BlockSpec((tm, n_main), lambda i: (i, 0)),
        out_shape=jax.ShapeDtypeStruct((T, n_main), BF16),
        compiler_params=_cparams("parallel"),
        name="norm_proj_na",
    )(x, gain, mod, w)


def _outproj_kernel(o_ref, w_ref, x_ref, g_ref, mod_ref, y_ref):
    y = _dot(o_ref[...], w_ref[...])
    gate = mod_ref[0][2:3]
    y_ref[...] = x_ref[...] + gate * _rms_gain(y, g_ref[...])


def _outproj(o, w, x, gain, mod, L):
    T, K = o.shape
    tm = min(ROW_TILE, L)
    spt = L // tm
    return pl.pallas_call(
        _outproj_kernel,
        grid=(T // tm,),
        in_specs=[pl.BlockSpec((tm, K), lambda i: (i, 0)),
                  pl.BlockSpec((K, D_MODEL), lambda i: (0, 0)),
                  pl.BlockSpec((tm, D_MODEL), lambda i: (i, 0)),
                  pl.BlockSpec((1, D_MODEL), lambda i: (0, 0)),
                  pl.BlockSpec((1, 6, D_MODEL), lambda i: (i // spt, 0, 0))],
        out_specs=pl.BlockSpec((tm, D_MODEL), lambda i: (i, 0)),
        out_shape=jax.ShapeDtypeStruct((T, D_MODEL), F32),
        compiler_params=_cparams("parallel"),
        name="outproj_residual",
    )(o, w, x, gain, mod)


def _ffn_kernel(x_ref, g2_ref, g3_ref, mod_ref, wg_ref, wu_ref, wd_ref, y_ref, h_ref, acc_ref):
    f = pl.program_id(1)
    m = mod_ref[0]

    @pl.when(f == 0)
    def _():
        h_ref[...] = _norm_mod(x_ref[...], g2_ref[...], m[4:5], m[3:4]).astype(BF16)
        acc_ref[...] = jnp.zeros_like(acc_ref)

    h = h_ref[...]
    a = (_silu(_dot(h, wg_ref[...])) * _dot(h, wu_ref[...])).astype(BF16)
    acc_ref[...] += _dot(a, wd_ref[...])

    @pl.when(f == pl.num_programs(1) - 1)
    def _():
        y_ref[...] = x_ref[...] + m[5:6] * _rms_gain(acc_ref[...], g3_ref[...])


def _ffn(x, gain2, gain3, mod, wg, wu, wd, L):
    T = x.shape[0]
    tm = min(FFN_ROW_TILE, L)
    spt = L // tm
    tf = FFN_COL_TILE
    return pl.pallas_call(
        _ffn_kernel,
        grid=(T // tm, D_FF // tf),
        in_specs=[pl.BlockSpec((tm, D_MODEL), lambda i, f: (i, 0)),
                  pl.BlockSpec((1, D_MODEL), lambda i, f: (0, 0)),
                  pl.BlockSpec((1, D_MODEL), lambda i, f: (0, 0)),
                  pl.BlockSpec((1, 6, D_MODEL), lambda i, f: (i // spt, 0, 0)),
                  pl.BlockSpec((D_MODEL, tf), lambda i, f: (0, f)),
                  pl.BlockSpec((D_MODEL, tf), lambda i, f: (0, f)),
                  pl.BlockSpec((tf, D_MODEL), lambda i, f: (f, 0))],
        out_specs=pl.BlockSpec((tm, D_MODEL), lambda i, f: (i, 0)),
        out_shape=jax.ShapeDtypeStruct((T, D_MODEL), F32),
        scratch_shapes=[pltpu.VMEM((tm, D_MODEL), BF16), pltpu.VMEM((tm, D_MODEL), F32)],
        compiler_params=_cparams("parallel", "arbitrary"),
        name="ffn_swiglu",
    )(x, gain2, gain3, mod, wg, wu, wd)


def _gla_block(q_ref, k_ref, v_ref, z_ref, S, *, tb, reverse):
    C = GLA_CHUNK
    chunks = list(range(tb // C))
    order = chunks[::-1] if reverse else chunks
    sl = {c: slice(c * C, (c + 1) * C) for c in chunks}
    row = lax.broadcasted_iota(jnp.int32, (C, C), 0)
    col = lax.broadcasted_iota(jnp.int32, (C, C), 1)
    if reverse:
        tri = col >= row
        allowed = col > row
        i_ref, i_last = C - 1 - C // 2, 0
    else:
        tri = row >= col
        allowed = row >= col
        i_ref, i_last = C // 2, C - 1
    tri = jnp.where(tri, 1.0, 0.0).astype(BF16)
    dk = q_ref.shape[1]
    reps = v_ref.shape[1] // dk

    z = z_ref[...]
    g = (jnp.minimum(z, 0.0) - jnp.log(1.0 + jnp.exp(-jnp.abs(z)))) * (1.0 / GLA_TAU)
    g_hi = g.astype(BF16)
    rest = g - g_hi.astype(F32)
    g_mid = rest.astype(BF16)
    g_lo = (rest - g_mid.astype(F32)).astype(BF16)
    g_terms = jnp.concatenate([g_hi, g_mid, g_lo], axis=1)
    G = {}
    for c in chunks:
        t = _dot(tri, g_terms[sl[c]])
        G[c] = t[:, :dk] + t[:, dk:2 * dk] + t[:, 2 * dk:]
    q_rel, k_rel, q_in, k_out, decay = {}, {}, {}, {}, {}
    for c in chunks:
        q = q_ref[sl[c], :].astype(F32)
        k = k_ref[sl[c], :].astype(F32)
        G_ref = G[c][i_ref:i_ref + 1]
        G_last = G[c][i_last:i_last + 1]
        q_rel[c] = (q * jnp.exp(G[c] - G_ref)).astype(BF16)
        k_rel[c] = (k * jnp.exp(G_ref - G[c])).astype(BF16)
        q_in[c] = (q * jnp.exp(G[c])).astype(BF16)
        k_out[c] = (k * jnp.exp(G_last - G[c])).astype(BF16)
        d = jnp.broadcast_to(jnp.exp(G_last), (dk, dk)).T
        decay[c] = jnp.concatenate([d] * reps, axis=1)
    scores = {c: jnp.where(allowed, _dot_nt(q_rel[c], k_rel[c]), 0.0).astype(BF16) for c in chunks}
    kv = {c: _dot_tn(k_out[c], v_ref[sl[c], :]) for c in chunks}
    S_in = {}
    for c in order:
        S_in[c] = S.astype(BF16)
        S = decay[c] * S + kv[c]
    o = {c: _dot(scores[c], v_ref[sl[c], :]) + _dot(q_in[c], S_in[c]) for c in chunks}
    return o, S


def _gla_fwd_kernel(q_ref, k_ref, v_ref, z_ref, o_ref, s_ref, *, tb):
    @pl.when(pl.program_id(2) == 0)
    def _():
        s_ref[...] = jnp.zeros_like(s_ref)

    o, S = _gla_block(q_ref, k_ref, v_ref, z_ref, s_ref[...], tb=tb, reverse=False)
    for c, oc in o.items():
        o_ref[c * GLA_CHUNK:(c + 1) * GLA_CHUNK, :] = oc
    s_ref[...] = S


def _gla_bwd_kernel(q_ref, k_ref, v_ref, z_ref, of_ref, r_ref, gain_ref, o_ref, s_ref, *, tb):
    @pl.when(pl.program_id(2) == 0)
    def _():
        s_ref[...] = jnp.zeros_like(s_ref)

    o, S = _gla_block(q_ref, k_ref, v_ref, z_ref, s_ref[...], tb=tb, reverse=True)
    for c, oc in o.items():
        sl = slice(c * GLA_CHUNK, (c + 1) * GLA_CHUNK)
        oc = _rms_gain(oc + of_ref[sl, :], gain_ref[...])
        o_ref[sl, :] = (oc * _silu(r_ref[sl, :].astype(F32))).astype(o_ref.dtype)
    s_ref[...] = S


def _gla_core(main, z, gla_norm, B, L):
    T = B * L
    tb = min(GLA_TIME_BLOCK, L)
    nt = L // tb
    H = GLA_HEADS
    dk, dv = GLA_DK_HEAD, GLA_DV_HEAD
    grid = (B, H, nt)
    scratch = [pltpu.VMEM((dk, dv), F32)]

    def fwd_t(b, h, t):
        return b * nt + t

    def bwd_t(b, h, t):
        return b * nt + (nt - 1 - t)

    o_fwd = pl.pallas_call(
        functools.partial(_gla_fwd_kernel, tb=tb),
        grid=grid,
        in_specs=[pl.BlockSpec((tb, dk), lambda b, h, t: (fwd_t(b, h, t), h)),
                  pl.BlockSpec((tb, dk), lambda b, h, t: (fwd_t(b, h, t), H + h)),
                  pl.BlockSpec((tb, dv), lambda b, h, t: (fwd_t(b, h, t), H + h)),
                  pl.BlockSpec((tb, dk), lambda b, h, t: (fwd_t(b, h, t), h))],
        out_specs=pl.BlockSpec((tb, dv), lambda b, h, t: (fwd_t(b, h, t), h)),
        out_shape=jax.ShapeDtypeStruct((T, GLA_DV), F32),
        scratch_shapes=scratch,
        compiler_params=_cparams("parallel", "parallel", "arbitrary"),
        name="gla_forward",
    )(main, main, main, z)
    return pl.pallas_call(
        functools.partial(_gla_bwd_kernel, tb=tb),
        grid=grid,
        in_specs=[pl.BlockSpec((tb, dk), lambda b, h, t: (bwd_t(b, h, t), h)),
                  pl.BlockSpec((tb, dk), lambda b, h, t: (bwd_t(b, h, t), H + h)),
                  pl.BlockSpec((tb, dv), lambda b, h, t: (bwd_t(b, h, t), H + h)),
                  pl.BlockSpec((tb, dk), lambda b, h, t: (bwd_t(b, h, t), H + h)),
                  pl.BlockSpec((tb, dv), lambda b, h, t: (bwd_t(b, h, t), h)),
                  pl.BlockSpec((tb, dv), lambda b, h, t: (bwd_t(b, h, t), 2 * H + h)),
                  pl.BlockSpec((1, dv), lambda b, h, t: (0, h))],
        out_specs=pl.BlockSpec((tb, dv), lambda b, h, t: (bwd_t(b, h, t), h)),
        out_shape=jax.ShapeDtypeStruct((T, GLA_DV), BF16),
        scratch_shapes=scratch,
        compiler_params=_cparams("parallel", "parallel", "arbitrary"),
        name="gla_backward_finish",
    )(main, main, main, z, o_fwd, main, gla_norm)


def _na_bias_table(rpb):
    cols = jnp.arange(GRID_W)
    col_start = jnp.clip(cols - WIN_COLS // 2, 0, GRID_W - WIN_COLS)
    in_win = (cols[None, :] >= col_start[:, None]) & (cols[None, :] < col_start[:, None] + WIN_COLS)
    dc_idx = jnp.clip(cols[None, :] - cols[:, None] + WIN_COLS - 1, 0, 2 * WIN_COLS - 2)
    pick_c = (dc_idx[:, :, None] == jnp.arange(2 * WIN_COLS - 1)).astype(F32)
    a = jnp.arange(NA_Q_ROWS)
    j = jnp.arange(NA_K_ROWS)
    tables = []
    for delta, win_start in ((0, jnp.zeros_like(a)), (4, a), (8, jnp.full_like(a, 4))):
        live_row = (j[None, :] >= win_start[:, None]) & (j[None, :] < win_start[:, None] + WIN_ROWS)
        dr_idx = jnp.clip(j[None, :] - delta - a[:, None] + WIN_ROWS - 1, 0, 2 * WIN_ROWS - 2)
        pick_r = (dr_idx[:, :, None] == jnp.arange(2 * WIN_ROWS - 1)).astype(F32)
        bias = jnp.einsum('hrc,ajr,qkc->haqjk', rpb.astype(F32), pick_r, pick_c,
                          precision=lax.Precision.HIGHEST) * LOG2E
        live = live_row[:, None, :, None] & in_win[None, :, None, :]
        tables.append(jnp.where(live[None], bias, NEG_INF))
    t = jnp.stack(tables)
    return t.reshape(3, NA_HEADS, NA_Q_ROWS * GRID_W, NA_K_ROWS * GRID_W)


def _na_kernel(q_ref, k_ref, v_ref, bias_ref, o_ref, *, rows):
    ntile = rows // NA_Q_ROWS
    tq = NA_Q_ROWS * GRID_W
    tk = NA_K_ROWS * GRID_W
    lane = lax.broadcasted_iota(jnp.int32, (1, LANES), 1)
    first_head = lane < NA_HEAD_DIM

    def body(tt, carry):
        q0, geo, k, v, s = {}, {}, {}, {}, {}
        units = [(u, hh) for u in range(NA_TILES_PER_STEP) for hh in range(2)]
        for u in range(NA_TILES_PER_STEP):
            t = NA_TILES_PER_STEP * tt + u
            kr0 = jnp.clip(NA_Q_ROWS * t - 4, 0, rows - NA_K_ROWS)
            geo[u] = jnp.where(t == 0, 0, jnp.where(t == ntile - 1, 2, 1))
            q0[u] = pl.multiple_of(t * tq, tq)
            k0 = pl.multiple_of(kr0 * GRID_W, GRID_W)
            q = q_ref[pl.ds(q0[u], tq), :]
            k[u] = k_ref[pl.ds(k0, tk), :]
            v[u] = v_ref[pl.ds(k0, tk), :]
            for hh in range(2):
                sel = first_head if hh == 0 else jnp.logical_not(first_head)
                s[u, hh] = _dot_nt(jnp.where(sel, q, jnp.zeros_like(q)), k[u])
        p, inv_l, o = {}, {}, {}
        for u, hh in units:
            sb = s[u, hh] + bias_ref[geo[u], hh]
            e = jnp.exp2(sb - jnp.max(sb, axis=-1, keepdims=True))
            inv_l[u, hh] = 1.0 / jnp.sum(e, axis=-1, keepdims=True)
            p[u, hh] = e.astype(BF16)
        for u, hh in units:
            o[u, hh] = _dot(p[u, hh], v[u]) * inv_l[u, hh]
        for u in range(NA_TILES_PER_STEP):
            o_ref[pl.ds(q0[u], tq), :] = jnp.where(first_head, o[u, 0], o[u, 1]).astype(o_ref.dtype)
        return carry

    lax.fori_loop(0, ntile // NA_TILES_PER_STEP, body, 0)


def _na_core(qkv, bias, B, L):
    T = B * L
    rows = L // GRID_W
    npair = NA_HEADS // 2
    tq = NA_Q_ROWS * GRID_W
    tk = NA_K_ROWS * GRID_W
    assert (rows // NA_Q_ROWS) % NA_TILES_PER_STEP == 0 and rows >= NA_K_ROWS
    return pl.pallas_call(
        functools.partial(_na_kernel, rows=rows),
        grid=(npair, B),
        in_specs=[pl.BlockSpec((L, LANES), lambda p, b: (b, p)),
                  pl.BlockSpec((L, LANES), lambda p, b: (b, npair + p)),
                  pl.BlockSpec((L, LANES), lambda p, b: (b, 2 * npair + p)),
                  pl.BlockSpec((3, 2, tq, tk), lambda p, b: (0, p, 0, 0))],
        out_specs=pl.BlockSpec((L, LANES), lambda p, b: (b, p)),
        out_shape=jax.ShapeDtypeStruct((T, D_MODEL), BF16),
        compiler_params=_cparams("parallel", "parallel"),
        name="neighbourhood_attention",
    )(qkv, qkv, qkv, bias)


def _router_kernel(x_ref, g_ref, mod_ref, wr_ref, h_ref, route_ref, cnt_ref, carry_ref):
    @pl.when(pl.program_id(0) == 0)
    def _():
        carry_ref[...] = jnp.zeros_like(carry_ref)

    m = mod_ref[0]
    h = _norm_mod(x_ref[...], g_ref[...], m[4:5], m[3:4])
    tm = h.shape[0]
    for s in range(SUBLANES):
        h_ref[pl.ds(s, tm, stride=SUBLANES), :] = h[:, s * LANES:(s + 1) * LANES]
    h_hi = h.astype(BF16)
    h_lo = (h - h_hi.astype(F32)).astype(BF16)
    both = _dot(h_hi, wr_ref[...])
    logits = both[:, :LANES] + both[:, LANES:] + _dot(h_lo, wr_ref[:, :LANES])
    lane = lax.broadcasted_iota(jnp.int32, logits.shape, 1)
    valid = lane < N_EXPERTS
    l1 = jnp.where(valid, logits, -jnp.inf)
    m1 = jnp.max(l1, axis=-1, keepdims=True)
    i1 = jnp.min(jnp.where(l1 == m1, lane, LANES), axis=-1, keepdims=True)
    l2 = jnp.where(lane == i1, -jnp.inf, l1)
    m2 = jnp.max(l2, axis=-1, keepdims=True)
    i2 = jnp.min(jnp.where(l2 == m2, lane, LANES), axis=-1, keepdims=True)
    e = jnp.exp(m2 - m1)
    g1 = 1.0 / (1.0 + e)
    g2 = e * g1
    chosen = (lane == i1) | (lane == i2)
    carry_ref[...] = carry_ref[...] + jnp.sum(jnp.where(chosen, 1.0, 0.0), axis=0, keepdims=True)
    cnt_ref[...] = carry_ref[...]
    route = jnp.where(lane == 0, i1.astype(F32),
                      jnp.where(lane == 1, i2.astype(F32),
                                jnp.where(lane == 2, g1, jnp.where(lane == 3, g2, 0.0))))
    route_ref[...] = route


def _router(x, gain, mod, wr, L):
    T = x.shape[0]
    tm = min(ROW_TILE, L)
    spt = L // tm
    return pl.pallas_call(
        _router_kernel,
        grid=(T // tm,),
        in_specs=[pl.BlockSpec((tm, D_MODEL), lambda i: (i, 0)),
                  pl.BlockSpec((1, D_MODEL), lambda i: (0, 0)),
                  pl.BlockSpec((1, 6, D_MODEL), lambda i: (i // spt, 0, 0)),
                  pl.BlockSpec((D_MODEL, 2 * LANES), lambda i: (0, 0))],
        out_specs=[pl.BlockSpec((tm * SUBLANES, LANES), lambda i: (i, 0)),
                   pl.BlockSpec((tm, LANES), lambda i: (i, 0)),
                   pl.BlockSpec((SUBLANES, LANES), lambda i: (0, 0))],
        out_shape=[jax.ShapeDtypeStruct((T * SUBLANES, LANES), F32),
                   jax.ShapeDtypeStruct((T, LANES), F32),
                   jax.ShapeDtypeStruct((SUBLANES, LANES), F32)],
        scratch_shapes=[pltpu.VMEM((SUBLANES, LANES), F32)],
        compiler_params=_cparams("arbitrary"),
        name="moe_router",
    )(x, gain, mod, wr)


def _moe_kernel(be_ref, srcx_hbm, dstx_hbm, h_hbm, wg_ref, wu_ref, wd_ref, out_hbm,
                src_smem, dst_smem, xbuf, xb16, acc_ref, ybuf,
                src_sem, dst_sem, gat_sem, sca_sem, *, bm):
    b = pl.program_id(0)
    f = pl.program_id(1)
    nb = pl.num_programs(0)
    nf = pl.num_programs(1)
    slot = b % 2
    nxt = 1 - slot
    group = -(-bm // MOE_ISSUE_STEPS)
    always = bm - (MOE_ISSUE_STEPS - 1) * group

    def src_copy(r, s):
        return pltpu.make_async_copy(srcx_hbm.at[r], src_smem.at[s], src_sem.at[s])

    def dst_copy(r, s):
        return pltpu.make_async_copy(dstx_hbm.at[r], dst_smem.at[s], dst_sem.at[s])

    def tile(ref, i):
        return ref.at[pl.ds(pl.multiple_of(i * SUBLANES, SUBLANES), SUBLANES)]

    def gather_row(i, s):
        return pltpu.make_async_copy(tile(h_hbm, src_smem[s, i]), tile(xbuf.at[s], i), gat_sem.at[s])

    def scatter_row(i, s):
        return pltpu.make_async_copy(tile(ybuf, i), tile(out_hbm, dst_smem[s, i]), sca_sem)

    def wait_gather(s):
        pltpu.make_async_copy(h_hbm.at[pl.ds(0, bm * SUBLANES)], xbuf.at[s], gat_sem.at[s]).wait()

    def wait_scatter():
        pltpu.make_async_copy(ybuf, out_hbm.at[pl.ds(0, bm * SUBLANES)], sca_sem).wait()

    def for_all_rows(start_row):
        def issue(i, c):
            start_row(i)
            return c
        lax.fori_loop(0, bm, issue, 0)

    @pl.when(f == 0)
    def _():
        @pl.when(b == 0)
        def _():
            src_copy(0, 0).start()
            src_copy(1, 1).start()
            dst_copy(0, 0).start()
            src_copy(0, 0).wait()
            for_all_rows(lambda i: gather_row(i, 0).start())
            ybuf[...] = jnp.zeros_like(ybuf)

        src_copy(b + 1, nxt).wait()
        dst_copy(b, slot).wait()
        wait_gather(slot)
        for s in range(SUBLANES):
            xb16[:, s * LANES:(s + 1) * LANES] = xbuf[slot, pl.ds(s, bm, stride=SUBLANES), :].astype(BF16)
        acc_ref[...] = jnp.zeros_like(acc_ref)

    base = f * group
    for j in range(always):
        gather_row(base + j, nxt).start()
        scatter_row(base + j, slot).start()

    x = xb16[...]
    a = (_silu(_dot(x, wg_ref[0])) * _dot(x, wu_ref[0])).astype(BF16)
    acc_ref[...] += _dot(a, wd_ref[0])

    @pl.when(f < nf - 1)
    def _():
        for j in range(always, group):
            gather_row(base + j, nxt).start()
            scatter_row(base + j, slot).start()

    @pl.when(f == nf - 1)
    def _():
        wait_scatter()
        for s in range(SUBLANES):
            ybuf[pl.ds(s, bm, stride=SUBLANES), :] = acc_ref[:, s * LANES:(s + 1) * LANES]
        dst_copy(b + 1, nxt).start()

        @pl.when(b + 2 <= nb)
        def _():
            src_copy(b + 2, slot).start()

        @pl.when(b == nb - 1)
        def _():
            dst_copy(b + 1, nxt).wait()
            for_all_rows(lambda i: scatter_row(i, nxt).start())
            wait_scatter()
            wait_gather(nxt)


def _moe_experts(block_e, srcx, dstx, h, wg, wu, wd, n_rows_out):
    n_blocks = srcx.shape[0] - 1
    bm = srcx.shape[1]
    tf = FFN_COL_TILE
    assert D_FF // tf == MOE_ISSUE_STEPS
    grid_spec = pltpu.PrefetchScalarGridSpec(
        num_scalar_prefetch=1,
        grid=(n_blocks, D_FF // tf),
        in_specs=[pl.BlockSpec(memory_space=pl.ANY),
                  pl.BlockSpec(memory_space=pl.ANY),
                  pl.BlockSpec(memory_space=pl.ANY),
                  pl.BlockSpec((1, D_MODEL, tf), lambda b, f, be: (be[b], 0, f)),
                  pl.BlockSpec((1, D_MODEL, tf), lambda b, f, be: (be[b], 0, f)),
                  pl.BlockSpec((1, tf, D_MODEL), lambda b, f, be: (be[b], f, 0))],
        out_specs=pl.BlockSpec(memory_space=pl.ANY),
        scratch_shapes=[pltpu.SMEM((2, bm), jnp.int32),
                        pltpu.SMEM((2, bm), jnp.int32),
                        pltpu.VMEM((2, bm * SUBLANES, LANES), F32),
                        pltpu.VMEM((bm, D_MODEL), BF16),
                        pltpu.VMEM((bm, D_MODEL), F32),
                        pltpu.VMEM((bm * SUBLANES, LANES), F32),
                        pltpu.SemaphoreType.DMA((2,)),
                        pltpu.SemaphoreType.DMA((2,)),
                        pltpu.SemaphoreType.DMA((2,)),
                        pltpu.SemaphoreType.DMA(())],
    )
    return pl.pallas_call(
        functools.partial(_moe_kernel, bm=bm),
        grid_spec=grid_spec,
        out_shape=jax.ShapeDtypeStruct((n_rows_out * SUBLANES, LANES), F32),
        compiler_params=_cparams("arbitrary", "arbitrary"),
        name="moe_experts",
    )(block_e, srcx, dstx, h, wg, wu, wd)


def _combine_kernel(y1_ref, y2_ref, route_ref, x_ref, g_ref, mod_ref, o_ref, y_ref):
    r = route_ref[...]
    tm = r.shape[0]
    for s in range(SUBLANES):
        rows = pl.ds(s, tm, stride=SUBLANES)
        y_ref[:, s * LANES:(s + 1) * LANES] = r[:, 2:3] * y1_ref[rows, :] + r[:, 3:4] * y2_ref[rows, :]
    gate = mod_ref[0][5:6]
    o_ref[...] = x_ref[...] + gate * _rms_gain(y_ref[...], g_ref[...])


def _moe_combine(y, route, x, gain, mod, L):
    T = x.shape[0]
    tm = min(ROW_TILE, L)
    spt = L // tm
    nt = T // tm
    return pl.pallas_call(
        _combine_kernel,
        grid=(nt,),
        in_specs=[pl.BlockSpec((tm * SUBLANES, LANES), lambda i: (i, 0)),
                  pl.BlockSpec((tm * SUBLANES, LANES), lambda i: (nt + i, 0)),
                  pl.BlockSpec((tm, LANES), lambda i: (i, 0)),
                  pl.BlockSpec((tm, D_MODEL), lambda i: (i, 0)),
                  pl.BlockSpec((1, D_MODEL), lambda i: (0, 0)),
                  pl.BlockSpec((1, 6, D_MODEL), lambda i: (i // spt, 0, 0))],
        out_specs=pl.BlockSpec((tm, D_MODEL), lambda i: (i, 0)),
        out_shape=jax.ShapeDtypeStruct((T, D_MODEL), F32),
        scratch_shapes=[pltpu.VMEM((tm, D_MODEL), F32)],
        compiler_params=_cparams("parallel"),
        name="moe_combine_residual",
    )(y, y, route, x, gain, mod)


def _moe_plan(route, counts, T, bm):
    n_assign = TOP_K * T
    n_blocks = n_assign // bm + N_EXPERTS
    experts = jnp.arange(N_EXPERTS, dtype=jnp.int32)
    e_flat = jnp.concatenate([route[:, k] for k in range(TOP_K)]).astype(jnp.int32)
    _, order = lax.sort_key_val(e_flat, jnp.arange(n_assign, dtype=jnp.int32))
    counts = counts[0, :N_EXPERTS].astype(jnp.int32)
    padded = ((counts + bm - 1) // bm) * bm
    start = jnp.cumsum(counts) - counts
    pend = jnp.cumsum(padded)
    pstart = pend - padded
    pos = jnp.arange(n_blocks * bm, dtype=jnp.int32)
    pe = jnp.minimum(jnp.sum(pos[:, None] >= pend[None, :], axis=1), N_EXPERTS - 1).astype(jnp.int32)
    mine = pe[:, None] == experts[None, :]

    def of_expert(table):
        return jnp.sum(jnp.where(mine, table[None, :], 0), axis=1).astype(jnp.int32)

    rank = pos - of_expert(pstart)
    valid = rank < of_expert(counts)
    a = jnp.take(order, jnp.clip(of_expert(start) + rank, 0, n_assign - 1))
    spare = (n_assign + pos % bm).astype(jnp.int32)
    dst = jnp.where(valid, a, spare)
    src = jnp.where(valid, a % T, 0)
    block_e = pe[::bm]
    srcx = jnp.concatenate([src.reshape(n_blocks, bm), jnp.zeros((1, bm), jnp.int32)], axis=0)
    dstx = jnp.concatenate([spare[:bm].reshape(1, bm), dst.reshape(n_blocks, bm)], axis=0)
    return block_e, srcx, dstx


def _moe_layer(x, gain2, gain3, mod, wr, wg, wu, wd, L):
    T = x.shape[0]
    bm = min(MOE_BLOCK, T)
    h, route, counts = _router(x, gain2, mod, wr, L)
    block_e, srcx, dstx = _moe_plan(route, counts, T, bm)
    y = _moe_experts(block_e, srcx, dstx, h, wg, wu, wd, TOP_K * T + bm)
    return _moe_combine(y, route, x, gain3, mod, L)


def _prep_weights(gla_w_in, gla_w_gate_up, gla_b_gate, gla_norm, gla_w_out, na_w_qkv, na_rpb,
                  na_w_out, ffn_w_gate, ffn_w_up, ffn_w_down, moe_router, moe_w_gate, moe_w_up,
                  moe_w_down):
    n_a = gla_w_in.shape[0]
    n_b = na_w_qkv.shape[0]
    R = GLA_GATE_RANK
    n_qkvr = 2 * GLA_DK + 2 * GLA_DV
    prep = {'gla': [], 'na': [], 'ffn': [], 'moe': []}
    for j in range(n_a):
        w = gla_w_in[j]
        w1 = jnp.concatenate([w[:, n_qkvr:n_qkvr + R], w[:, n_qkvr + R:]], axis=0)
        w1 = jnp.pad(w1, ((0, 0), (0, LANES - R)))
        w2 = jnp.pad(gla_w_gate_up[j], ((0, 0), (0, LANES - R), (0, 0)))
        fold = _small_matmul(w1, w2, jnp.zeros((2, 1, GLA_DK), F32),
                             act=False, col_tile=GLA_DK)
        w_z = jnp.concatenate([fold[0, :D_MODEL], fold[1, D_MODEL:]], axis=1)
        w_q = w[:, :GLA_DK] * (GLA_DK_HEAD ** -0.5)
        w_all = jnp.concatenate([w_q, w[:, GLA_DK:n_qkvr], w_z], axis=1).astype(BF16)
        prep['gla'].append(dict(w_all=w_all, zb=gla_b_gate[j].reshape(1, 2 * GLA_DK),
                                norm=gla_norm[j].reshape(1, GLA_DV),
                                w_out=gla_w_out[j].astype(BF16)))
        prep['ffn'].append(dict(wg=ffn_w_gate[j].astype(BF16), wu=ffn_w_up[j].astype(BF16),
                                wd=ffn_w_down[j].astype(BF16)))
    for j in range(n_b):
        w = na_w_qkv[j]
        w_qkv = jnp.concatenate([w[:, :D_MODEL] * (NA_HEAD_DIM ** -0.5 * LOG2E), w[:, D_MODEL:]],
                                axis=1).astype(BF16)
        prep['na'].append(dict(w_qkv=w_qkv, bias=_na_bias_table(na_rpb[j]),
                               w_out=na_w_out[j].astype(BF16)))
        wr = jnp.pad(moe_router[j], ((0, 0), (0, LANES - N_EXPERTS)))
        wr_hi = wr.astype(BF16)
        wr = jnp.concatenate([wr_hi, (wr - wr_hi.astype(F32)).astype(BF16)], axis=1)
        prep['moe'].append(dict(wr=wr, wg=moe_w_gate[j].astype(BF16), wu=moe_w_up[j].astype(BF16),
                                wd=moe_w_down[j].astype(BF16)))
    return prep


def _trunk(x, mod_all, norm_gains, prep):
    B, L, _ = x.shape
    x = x.reshape(B * L, D_MODEL)
    for i in range(DEPTH):
        j = i // 2
        mod = mod_all[i]
        gains = [norm_gains[i, n].reshape(1, D_MODEL) for n in range(4)]
        if i % 2 == 0:
            p = prep['gla'][j]
            main, z = _proj(x, gains[0], mod, p['w_all'], p['zb'], L,
                            2 * GLA_DK + 2 * GLA_DV, 2 * GLA_DK)
            o = _gla_core(main, z, p['norm'], B, L)
            x = _outproj(o, p['w_out'], x, gains[1], mod, L)
            f = prep['ffn'][j]
            x = _ffn(x, gains[2], gains[3], mod, f['wg'], f['wu'], f['wd'], L)
        else:
            p = prep['na'][j]
            qkv = _proj(x, gains[0], mod, p['w_qkv'], None, L, 3 * D_MODEL, 0)
            o = _na_core(qkv, p['bias'], B, L)
            x = _outproj(o, p['w_out'], x, gains[1], mod, L)
            m = prep['moe'][j]
            x = _moe_layer(x, gains[2], gains[3], mod, m['wr'], m['wg'], m['wu'], m['wd'], L)
    return x.reshape(B, L, D_MODEL)


def kernel(x_prompt, x_sample, c_prompt, c_sample, ada_w, ada_b, norm_gains, gla_w_in, gla_w_gate_up, gla_b_gate, gla_norm, gla_w_out, na_w_qkv, na_rpb, na_w_out, ffn_w_gate, ffn_w_up, ffn_w_down, moe_router, moe_w_gate, moe_w_up, moe_w_down):
    prep = _prep_weights(gla_w_in, gla_w_gate_up, gla_b_gate, gla_norm, gla_w_out, na_w_qkv, na_rpb,
                         na_w_out, ffn_w_gate, ffn_w_up, ffn_w_down, moe_router, moe_w_gate,
                         moe_w_up, moe_w_down)
    bp, bs = c_prompt.shape[0], c_sample.shape[0]
    rows = -(-(bp + bs) // 8) * 8
    c = jnp.zeros((rows, D_MODEL), F32).at[:bp].set(c_prompt).at[bp:bp + bs].set(c_sample)
    mod = _small_matmul(c, ada_w, ada_b.reshape(DEPTH, 1, 6 * D_MODEL), act=True, col_tile=D_MODEL)
    mod_p = mod[:, :bp].reshape(DEPTH, bp, 6, D_MODEL)
    mod_s = mod[:, bp:bp + bs].reshape(DEPTH, bs, 6, D_MODEL)
    y_prompt = _trunk(x_prompt, mod_p, norm_gains, prep)
    y_sample = _trunk(x_sample, mod_s, norm_gains, prep)
    return (y_prompt, y_sample)
```

```python
import functools

import jax
import jax.numpy as jnp
from jax import lax
from jax.experimental import pallas as pl
from jax.experimental.pallas import tpu as pltpu

D_MODEL = 1024
DEPTH = 4
GRID_W = 64
GLA_HEADS = 4
GLA_DK = D_MODEL // 2
GLA_DV = D_MODEL
GLA_DK_HEAD = GLA_DK // GLA_HEADS
GLA_DV_HEAD = GLA_DV // GLA_HEADS
GLA_GATE_RANK = 16
GLA_TAU = 16.0
GLA_CHUNK = 64
NA_HEADS = 16
NA_HEAD_DIM = D_MODEL // NA_HEADS
WIN_ROWS = 8
WIN_COLS = 16
D_FF = 7 * D_MODEL // 2
N_EXPERTS = 8
TOP_K = 2
NORM_EPS = 1e-6
NEG_INF = -1e30
LOG2E = 1.4426950408889634

BF16 = jnp.bfloat16
F32 = jnp.float32

VMEM_LIMIT_BYTES = 56 * 1024 * 1024
LANES = 128
SUBLANES = 8

ROW_TILE = 512
FFN_ROW_TILE = 1024
FFN_COL_TILE = 512
GLA_TIME_BLOCK = 1024
NA_Q_ROWS = 4
NA_K_ROWS = 12
NA_TILES_PER_STEP = 4
MOE_BLOCK = 1024
MOE_ISSUE_STEPS = D_FF // FFN_COL_TILE


def _cparams(*sem):
    return pltpu.CompilerParams(dimension_semantics=sem, vmem_limit_bytes=VMEM_LIMIT_BYTES)


def _norm_mod(x, gain, scale, shift):
    ms = jnp.mean(x * x, axis=-1, keepdims=True)
    y = x * lax.rsqrt(ms + NORM_EPS) * gain
    return y * (1.0 + scale) + shift


def _rms_gain(y, gain):
    ms = jnp.mean(y * y, axis=-1, keepdims=True)
    return y * lax.rsqrt(ms + NORM_EPS) * gain


def _dot(a, b):
    return jnp.dot(a, b, preferred_element_type=F32)


def _dot_nt(a, b):
    return lax.dot_general(a, b, (((1,), (1,)), ((), ())), preferred_element_type=F32)


def _dot_tn(a, b):
    return lax.dot_general(a, b, (((0,), (0,)), ((), ())), preferred_element_type=F32)


def _silu(x):
    return x * (1.0 / (1.0 + jnp.exp(-x)))


def _small_matmul_kernel(a_ref, w_ref, b_ref, o_ref, *, act):
    a = a_ref[...]
    if act:
        a = _silu(a)
    o_ref[0] = jnp.dot(a, w_ref[0], preferred_element_type=F32,
                       precision=lax.Precision.HIGHEST) + b_ref[0]


def _small_matmul(a, w, b, *, act, col_tile):
    M, K = a.shape
    G, _, N = w.shape
    return pl.pallas_call(
        functools.partial(_small_matmul_kernel, act=act),
        grid=(G, N // col_tile),
        in_specs=[pl.BlockSpec((M, K), lambda g, j: (0, 0)),
                  pl.BlockSpec((1, K, col_tile), lambda g, j: (g, 0, j)),
                  pl.BlockSpec((1, 1, col_tile), lambda g, j: (g, 0, j))],
        out_specs=pl.BlockSpec((1, M, col_tile), lambda g, j: (g, 0, j)),
        out_shape=jax.ShapeDtypeStruct((G, M, N), F32),
        compiler_params=_cparams("parallel", "parallel"),
        name="small_matmul",
    )(a, w, b)


def _proj_kernel(x_ref, g_ref, mod_ref, w_ref, zb_ref, o_ref, z_ref, *, n_main, n_z, cn):
    m = mod_ref[0]
    h = _norm_mod(x_ref[...], g_ref[...], m[1:2], m[0:1]).astype(BF16)
    for n in range(0, n_main, cn):
        o_ref[:, n:n + cn] = _dot(h, w_ref[:, n:n + cn]).astype(o_ref.dtype)
    for n in range(0, n_z, cn):
        z_ref[:, n:n + cn] = _dot(h, w_ref[:, n_main + n:n_main + n + cn]) + zb_ref[:, n:n + cn]


def _proj_kernel_noz(x_ref, g_ref, mod_ref, w_ref, o_ref, *, n_main, cn):
    m = mod_ref[0]
    h = _norm_mod(x_ref[...], g_ref[...], m[1:2], m[0:1]).astype(BF16)
    for n in range(0, n_main, cn):
        o_ref[:, n:n + cn] = _dot(h, w_ref[:, n:n + cn]).astype(o_ref.dtype)


def _proj(x, gain, mod, w, zb, L, n_main, n_z):
    T = x.shape[0]
    tm = min(ROW_TILE, L)
    spt = L // tm
    N = n_main + n_z
    in_specs = [pl.BlockSpec((tm, D_MODEL), lambda i: (i, 0)),
                pl.BlockSpec((1, D_MODEL), lambda i: (0, 0)),
                pl.BlockSpec((1, 6, D_MODEL), lambda i: (i // spt, 0, 0)),
                pl.BlockSpec((D_MODEL, N), lambda i: (0, 0))]
    if n_z:
        return pl.pallas_call(
            functools.partial(_proj_kernel, n_main=n_main, n_z=n_z, cn=512),
            grid=(T // tm,),
            in_specs=in_specs + [pl.BlockSpec((1, n_z), lambda i: (0, 0))],
            out_specs=[pl.BlockSpec((tm, n_main), lambda i: (i, 0)),
                       pl.BlockSpec((tm, n_z), lambda i: (i, 0))],
            out_shape=[jax.ShapeDtypeStruct((T, n_main), BF16),
                       jax.ShapeDtypeStruct((T, n_z), F32)],
            compiler_params=_cparams("parallel"),
            name="norm_proj_gla",
        )(x, gain, mod, w, zb)
    return pl.pallas_call(
        functools.partial(_proj_kernel_noz, n_main=n_main, cn=512),
        grid=(T // tm,),
        in_specs=in_specs,
        out_specs=pl.BlockSpec((tm, n_main), lambda i: (i, 0)),
        out_shape=jax.ShapeDtypeStruct((T, n_main), BF16),
        compiler_params=_cparams("parallel"),
        name="norm_proj_na",
    )(x, gain, mod, w)


def _outproj_kernel(o_ref, w_ref, x_ref, g_ref, mod_ref, y_ref):
    y = _dot(o_ref[...], w_ref[...])
    gate = mod_ref[0][2:3]
    y_ref[...] = x_ref[...] + gate * _rms_gain(y, g_ref[...])


def _outproj(o, w, x, gain, mod, L):
    T, K = o.shape
    tm = min(ROW_TILE, L)
    spt = L // tm
    return pl.pallas_call(
        _outproj_kernel,
        grid=(T // tm,),
        in_specs=[pl.BlockSpec((tm, K), lambda i: (i, 0)),
                  pl.BlockSpec((K, D_MODEL), lambda i: (0, 0)),
                  pl.BlockSpec((tm, D_MODEL), lambda i: (i, 0)),
                  pl.BlockSpec((1, D_MODEL), lambda i: (0, 0)),
                  pl.BlockSpec((1, 6, D_MODEL), lambda i: (i // spt, 0, 0))],
        out_specs=pl.BlockSpec((tm, D_MODEL), lambda i: (i, 0)),
        out_shape=jax.ShapeDtypeStruct((T, D_MODEL), F32),
        compiler_params=_cparams("parallel"),
        name="outproj_residual",
    )(o, w, x, gain, mod)


def _ffn_kernel(o_ref, wo_ref, x_ref, g1_ref, g2_ref, g3_ref, mod_ref, wg_ref, wu_ref, wd_ref,
                y_ref, h_ref, acc_ref):
    f = pl.program_id(1)
    m = mod_ref[0]

    @pl.when(f == 0)
    def _():
        x1 = x_ref[...] + m[2:3] * _rms_gain(_dot(o_ref[...], wo_ref[...]), g1_ref[...])
        y_ref[...] = x1
        h_ref[...] = _norm_mod(x1, g2_ref[...], m[4:5], m[3:4]).astype(BF16)
        acc_ref[...] = jnp.zeros_like(acc_ref)

    h = h_ref[...]
    a = (_silu(_dot(h, wg_ref[...])) * _dot(h, wu_ref[...])).astype(BF16)
    acc_ref[...] += _dot(a, wd_ref[...])

    @pl.when(f == pl.num_programs(1) - 1)
    def _():
        y_ref[...] = y_ref[...] + m[5:6] * _rms_gain(acc_ref[...], g3_ref[...])


def _ffn(o, wo, x, gain1, gain2, gain3, mod, wg, wu, wd, L):
    T = x.shape[0]
    tm = min(FFN_ROW_TILE, L)
    spt = L // tm
    tf = FFN_COL_TILE
    return pl.pallas_call(
        _ffn_kernel,
        grid=(T // tm, D_FF // tf),
        in_specs=[pl.BlockSpec((tm, D_MODEL), lambda i, f: (i, 0)),
                  pl.BlockSpec((D_MODEL, D_MODEL), lambda i, f: (0, 0)),
                  pl.BlockSpec((tm, D_MODEL), lambda i, f: (i, 0)),
                  pl.BlockSpec((1, D_MODEL), lambda i, f: (0, 0)),
                  pl.BlockSpec((1, D_MODEL), lambda i, f: (0, 0)),
                  pl.BlockSpec((1, D_MODEL), lambda i, f: (0, 0)),
                  pl.BlockSpec((1, 6, D_MODEL), lambda i, f: (i // spt, 0, 0)),
                  pl.BlockSpec((D_MODEL, tf), lambda i, f: (0, f)),
                  pl.BlockSpec((D_MODEL, tf), lambda i, f: (0, f)),
                  pl.BlockSpec((tf, D_MODEL), lambda i, f: (f, 0))],
        out_specs=pl.BlockSpec((tm, D_MODEL), lambda i, f: (i, 0)),
        out_shape=jax.ShapeDtypeStruct((T, D_MODEL), F32),
        scratch_shapes=[pltpu.VMEM((tm, D_MODEL), BF16), pltpu.VMEM((tm, D_MODEL), F32)],
        compiler_params=_cparams("parallel", "arbitrary"),
        name="ffn_swiglu",
    )(o, wo, x, gain1, gain2, gain3, mod, wg, wu, wd)


def _gla_block(q_ref, k_ref, v_ref, z_ref, S, *, tb, reverse):
    C = GLA_CHUNK
    chunks = list(range(tb // C))
    order = chunks[::-1] if reverse else chunks
    sl = {c: slice(c * C, (c + 1) * C) for c in chunks}
    row = lax.broadcasted_iota(jnp.int32, (C, C), 0)
    col = lax.broadcasted_iota(jnp.int32, (C, C), 1)
    if reverse:
        tri = col >= row
        allowed = col > row
        i_ref, i_last = C - 1 - C // 2, 0
    else:
        tri = row >= col
        allowed = row >= col
        i_ref, i_last = C // 2, C - 1
    tri = jnp.where(tri, 1.0, 0.0).astype(BF16)
    dk = q_ref.shape[1]
    reps = v_ref.shape[1] // dk

    z = z_ref[...]
    g = (jnp.minimum(z, 0.0) - jnp.log(1.0 + jnp.exp(-jnp.abs(z)))) * (1.0 / GLA_TAU)
    g_hi = g.astype(BF16)
    rest = g - g_hi.astype(F32)
    g_mid = rest.astype(BF16)
    g_lo = (rest - g_mid.astype(F32)).astype(BF16)
    g_terms = jnp.concatenate([g_hi, g_mid, g_lo], axis=1)
    G = {}
    for c in chunks:
        t = _dot(tri, g_terms[sl[c]])
        G[c] = t[:, :dk] + t[:, dk:2 * dk] + t[:, 2 * dk:]
    q_rel, k_rel, q_in, k_out, decay = {}, {}, {}, {}, {}
    for c in chunks:
        q = q_ref[sl[c], :].astype(F32)
        k = k_ref[sl[c], :].astype(F32)
        G_ref = G[c][i_ref:i_ref + 1]
        G_last = G[c][i_last:i_last + 1]
        q_rel[c] = (q * jnp.exp(G[c] - G_ref)).astype(BF16)
        k_rel[c] = (k * jnp.exp(G_ref - G[c])).astype(BF16)
        q_in[c] = (q * jnp.exp(G[c])).astype(BF16)
        k_out[c] = (k * jnp.exp(G_last - G[c])).astype(BF16)
        d = jnp.broadcast_to(jnp.exp(G_last), (dk, dk)).T
        decay[c] = jnp.concatenate([d] * reps, axis=1)
    scores = {c: jnp.where(allowed, _dot_nt(q_rel[c], k_rel[c]), 0.0).astype(BF16) for c in chunks}
    kv = {c: _dot_tn(k_out[c], v_ref[sl[c], :]) for c in chunks}
    S_in = {}
    for c in order:
        S_in[c] = S.astype(BF16)
        S = decay[c] * S + kv[c]
    o = {c: _dot(scores[c], v_ref[sl[c], :]) + _dot(q_in[c], S_in[c]) for c in chunks}
    return o, S


def _gla_fwd_kernel(q_ref, k_ref, v_ref, z_ref, o_ref, s_ref, *, tb):
    @pl.when(pl.program_id(2) == 0)
    def _():
        s_ref[...] = jnp.zeros_like(s_ref)

    o, S = _gla_block(q_ref, k_ref, v_ref, z_ref, s_ref[...], tb=tb, reverse=False)
    for c, oc in o.items():
        o_ref[c * GLA_CHUNK:(c + 1) * GLA_CHUNK, :] = oc
    s_ref[...] = S


def _gla_bwd_kernel(q_ref, k_ref, v_ref, z_ref, of_ref, r_ref, gain_ref, o_ref, s_ref, *, tb):
    @pl.when(pl.program_id(2) == 0)
    def _():
        s_ref[...] = jnp.zeros_like(s_ref)

    o, S = _gla_block(q_ref, k_ref, v_ref, z_ref, s_ref[...], tb=tb, reverse=True)
    for c, oc in o.items():
        sl = slice(c * GLA_CHUNK, (c + 1) * GLA_CHUNK)
        oc = _rms_gain(oc + of_ref[sl, :], gain_ref[...])
        o_ref[sl, :] = (oc * _silu(r_ref[sl, :].astype(F32))).astype(o_ref.dtype)
    s_ref[...] = S


def _gla_core(main, z, gla_norm, B, L):
    T = B * L
    tb = min(GLA_TIME_BLOCK, L)
    nt = L // tb
    H = GLA_HEADS
    dk, dv = GLA_DK_HEAD, GLA_DV_HEAD
    grid = (B, H, nt)
    scratch = [pltpu.VMEM((dk, dv), F32)]

    def fwd_t(b, h, t):
        return b * nt + t

    def bwd_t(b, h, t):
        return b * nt + (nt - 1 - t)

    o_fwd = pl.pallas_call(
        functools.partial(_gla_fwd_kernel, tb=tb),
        grid=grid,
        in_specs=[pl.BlockSpec((tb, dk), lambda b, h, t: (fwd_t(b, h, t), h)),
                  pl.BlockSpec((tb, dk), lambda b, h, t: (fwd_t(b, h, t), H + h)),
                  pl.BlockSpec((tb, dv), lambda b, h, t: (fwd_t(b, h, t), H + h)),
                  pl.BlockSpec((tb, dk), lambda b, h, t: (fwd_t(b, h, t), h))],
        out_specs=pl.BlockSpec((tb, dv), lambda b, h, t: (fwd_t(b, h, t), h)),
        out_shape=jax.ShapeDtypeStruct((T, GLA_DV), F32),
        scratch_shapes=scratch,
        compiler_params=_cparams("parallel", "parallel", "arbitrary"),
        name="gla_forward",
    )(main, main, main, z)
    return pl.pallas_call(
        functools.partial(_gla_bwd_kernel, tb=tb),
        grid=grid,
        in_specs=[pl.BlockSpec((tb, dk), lambda b, h, t: (bwd_t(b, h, t), h)),
                  pl.BlockSpec((tb, dk), lambda b, h, t: (bwd_t(b, h, t), H + h)),
                  pl.BlockSpec((tb, dv), lambda b, h, t: (bwd_t(b, h, t), H + h)),
                  pl.BlockSpec((tb, dk), lambda b, h, t: (bwd_t(b, h, t), H + h)),
                  pl.BlockSpec((tb, dv), lambda b, h, t: (bwd_t(b, h, t), h)),
                  pl.BlockSpec((tb, dv), lambda b, h, t: (bwd_t(b, h, t), 2 * H + h)),
                  pl.BlockSpec((1, dv), lambda b, h, t: (0, h))],
        out_specs=pl.BlockSpec((tb, dv), lambda b, h, t: (bwd_t(b, h, t), h)),
        out_shape=jax.ShapeDtypeStruct((T, GLA_DV), BF16),
        scratch_shapes=scratch,
        compiler_params=_cparams("parallel", "parallel", "arbitrary"),
        name="gla_backward_finish",
    )(main, main, main, z, o_fwd, main, gla_norm)


def _na_bias_table(rpb):
    cols = jnp.arange(GRID_W)
    col_start = jnp.clip(cols - WIN_COLS // 2, 0, GRID_W - WIN_COLS)
    in_win = (cols[None, :] >= col_start[:, None]) & (cols[None, :] < col_start[:, None] + WIN_COLS)
    dc_idx = jnp.clip(cols[None, :] - cols[:, None] + WIN_COLS - 1, 0, 2 * WIN_COLS - 2)
    pick_c = (dc_idx[:, :, None] == jnp.arange(2 * WIN_COLS - 1)).astype(F32)
    a = jnp.arange(NA_Q_ROWS)
    j = jnp.arange(NA_K_ROWS)
    tables = []
    for delta, win_start in ((0, jnp.zeros_like(a)), (4, a), (8, jnp.full_like(a, 4))):
        live_row = (j[None, :] >= win_start[:, None]) & (j[None, :] < win_start[:, None] + WIN_ROWS)
        dr_idx = jnp.clip(j[None, :] - delta - a[:, None] + WIN_ROWS - 1, 0, 2 * WIN_ROWS - 2)
        pick_r = (dr_idx[:, :, None] == jnp.arange(2 * WIN_ROWS - 1)).astype(F32)
        bias = jnp.einsum('hrc,ajr,qkc->haqjk', rpb.astype(F32), pick_r, pick_c,
                          precision=lax.Precision.HIGHEST) * LOG2E
        live = live_row[:, None, :, None] & in_win[None, :, None, :]
        tables.append(jnp.where(live[None], bias, NEG_INF))
    t = jnp.stack(tables)
    return t.reshape(3, NA_HEADS, NA_Q_ROWS * GRID_W, NA_K_ROWS * GRID_W)


def _na_kernel(q_ref, k_ref, v_ref, bias_ref, o_ref, *, rows):
    ntile = rows // NA_Q_ROWS
    tq = NA_Q_ROWS * GRID_W
    tk = NA_K_ROWS * GRID_W
    lane = lax.broadcasted_iota(jnp.int32, (1, LANES), 1)
    first_head = lane < NA_HEAD_DIM

    def body(tt, carry):
        q0, geo, k, v, s = {}, {}, {}, {}, {}
        units = [(u, hh) for u in range(NA_TILES_PER_STEP) for hh in range(2)]
        for u in range(NA_TILES_PER_STEP):
            t = NA_TILES_PER_STEP * tt + u
            kr0 = jnp.clip(NA_Q_ROWS * t - 4, 0, rows - NA_K_ROWS)
            geo[u] = jnp.where(t == 0, 0, jnp.where(t == ntile - 1, 2, 1))
            q0[u] = pl.multiple_of(t * tq, tq)
            k0 = pl.multiple_of(kr0 * GRID_W, GRID_W)
            q = q_ref[pl.ds(q0[u], tq), :]
            k[u] = k_ref[pl.ds(k0, tk), :]
            v[u] = v_ref[pl.ds(k0, tk), :]
            for hh in range(2):
                sel = first_head if hh == 0 else jnp.logical_not(first_head)
                s[u, hh] = _dot_nt(jnp.where(sel, q, jnp.zeros_like(q)), k[u])
        p, inv_l, o = {}, {}, {}
        for u, hh in units:
            sb = s[u, hh] + bias_ref[geo[u], hh]
            e = jnp.exp2(sb - jnp.max(sb, axis=-1, keepdims=True))
            inv_l[u, hh] = 1.0 / jnp.sum(e, axis=-1, keepdims=True)
            p[u, hh] = e.astype(BF16)
        for u, hh in units:
            o[u, hh] = _dot(p[u, hh], v[u]) * inv_l[u, hh]
        for u in range(NA_TILES_PER_STEP):
            o_ref[pl.ds(q0[u], tq), :] = jnp.where(first_head, o[u, 0], o[u, 1]).astype(o_ref.dtype)
        return carry

    lax.fori_loop(0, ntile // NA_TILES_PER_STEP, body, 0)


def _na_core(qkv, bias, B, L):
    T = B * L
    rows = L // GRID_W
    npair = NA_HEADS // 2
    tq = NA_Q_ROWS * GRID_W
    tk = NA_K_ROWS * GRID_W
    assert (rows // NA_Q_ROWS) % NA_TILES_PER_STEP == 0 and rows >= NA_K_ROWS
    return pl.pallas_call(
        functools.partial(_na_kernel, rows=rows),
        grid=(npair, B),
        in_specs=[pl.BlockSpec((L, LANES), lambda p, b: (b, p)),
                  pl.BlockSpec((L, LANES), lambda p, b: (b, npair + p)),
                  pl.BlockSpec((L, LANES), lambda p, b: (b, 2 * npair + p)),
                  pl.BlockSpec((3, 2, tq, tk), lambda p, b: (0, p, 0, 0))],
        out_specs=pl.BlockSpec((L, LANES), lambda p, b: (b, p)),
        out_shape=jax.ShapeDtypeStruct((T, D_MODEL), BF16),
        compiler_params=_cparams("parallel", "parallel"),
        name="neighbourhood_attention",
    )(qkv, qkv, qkv, bias)


def _router_kernel(xa_ref, xb_ref, g_ref, mod_ref, wr_ref, h_ref, route_ref, cnt_ref, carry_ref, *,
                   na):
    i = pl.program_id(0)

    @pl.when(i == 0)
    def _():
        carry_ref[...] = jnp.zeros_like(carry_ref)

    m = mod_ref[0]
    x = jnp.where(i < na, xa_ref[...], xb_ref[...])
    h = _norm_mod(x, g_ref[...], m[4:5], m[3:4])
    tm = h.shape[0]
    for s in range(SUBLANES):
        h_ref[pl.ds(s, tm, stride=SUBLANES), :] = h[:, s * LANES:(s + 1) * LANES]
    h_hi = h.astype(BF16)
    h_lo = (h - h_hi.astype(F32)).astype(BF16)
    both = _dot(h_hi, wr_ref[...])
    logits = both[:, :LANES] + both[:, LANES:] + _dot(h_lo, wr_ref[:, :LANES])
    lane = lax.broadcasted_iota(jnp.int32, logits.shape, 1)
    valid = lane < N_EXPERTS
    l1 = jnp.where(valid, logits, -jnp.inf)
    m1 = jnp.max(l1, axis=-1, keepdims=True)
    i1 = jnp.min(jnp.where(l1 == m1, lane, LANES), axis=-1, keepdims=True)
    l2 = jnp.where(lane == i1, -jnp.inf, l1)
    m2 = jnp.max(l2, axis=-1, keepdims=True)
    i2 = jnp.min(jnp.where(l2 == m2, lane, LANES), axis=-1, keepdims=True)
    e = jnp.exp(m2 - m1)
    g1 = 1.0 / (1.0 + e)
    g2 = e * g1
    chosen = (lane == i1) | (lane == i2)
    carry_ref[...] = carry_ref[...] + jnp.sum(jnp.where(chosen, 1.0, 0.0), axis=0, keepdims=True)
    cnt_ref[...] = carry_ref[...]
    route = jnp.where(lane == 0, i1.astype(F32),
                      jnp.where(lane == 1, i2.astype(F32),
                                jnp.where(lane == 2, g1, jnp.where(lane == 3, g2, 0.0))))
    route_ref[...] = route


class _TwoTrunks:
    def __init__(self, xa, xb, La, Lb, tm):
        self.na = xa.shape[0] // tm
        self.nb = xb.shape[0] // tm
        self.T = xa.shape[0] + xb.shape[0]
        self.tm = tm
        seqs_a = xa.shape[0] // La
        tiles_per_seq_a = La // tm
        tiles_per_seq_b = Lb // tm
        na = self.na
        self.a_tile = lambda i: (jnp.minimum(i, na - 1), 0)
        self.b_tile = lambda i: (jnp.maximum(i - na, 0), 0)
        self.seq = lambda i: (jnp.where(i < na, i // tiles_per_seq_a,
                                        seqs_a + (i - na) // tiles_per_seq_b), 0, 0)


def _router(xa, xb, gain, mod, wr, La, Lb):
    tm = min(ROW_TILE, La, Lb)
    tt = _TwoTrunks(xa, xb, La, Lb, tm)
    T = tt.T
    return pl.pallas_call(
        functools.partial(_router_kernel, na=tt.na),
        grid=(T // tm,),
        in_specs=[pl.BlockSpec((tm, D_MODEL), tt.a_tile),
                  pl.BlockSpec((tm, D_MODEL), tt.b_tile),
                  pl.BlockSpec((1, D_MODEL), lambda i: (0, 0)),
                  pl.BlockSpec((1, 6, D_MODEL), tt.seq),
                  pl.BlockSpec((D_MODEL, 2 * LANES), lambda i: (0, 0))],
        out_specs=[pl.BlockSpec((tm * SUBLANES, LANES), lambda i: (i, 0)),
                   pl.BlockSpec((tm, LANES), lambda i: (i, 0)),
                   pl.BlockSpec((SUBLANES, LANES), lambda i: (0, 0))],
        out_shape=[jax.ShapeDtypeStruct((T * SUBLANES, LANES), F32),
                   jax.ShapeDtypeStruct((T, LANES), F32),
                   jax.ShapeDtypeStruct((SUBLANES, LANES), F32)],
        scratch_shapes=[pltpu.VMEM((SUBLANES, LANES), F32)],
        compiler_params=_cparams("arbitrary"),
        name="moe_router",
    )(xa, xb, gain, mod, wr)


def _moe_kernel(be_ref, nused_ref, srcx_hbm, dstx_hbm, h_hbm, wg_ref, wu_ref, wd_ref, out_hbm,
                src_smem, dst_smem, xbuf, xb16, acc_ref, ybuf,
                src_sem, dst_sem, gat_sem, sca_sem, *, bm):
    b = pl.program_id(0)
    f = pl.program_id(1)
    nb = pl.num_programs(0)
    nf = pl.num_programs(1)
    slot = b % 2
    nxt = 1 - slot
    group = -(-bm // MOE_ISSUE_STEPS)
    always = bm - (MOE_ISSUE_STEPS - 1) * group

    def src_copy(r, s):
        return pltpu.make_async_copy(srcx_hbm.at[r], src_smem.at[s], src_sem.at[s])

    def dst_copy(r, s):
        return pltpu.make_async_copy(dstx_hbm.at[r], dst_smem.at[s], dst_sem.at[s])

    def tile(ref, i):
        return ref.at[pl.ds(pl.multiple_of(i * SUBLANES, SUBLANES), SUBLANES)]

    def gather_row(i, s):
        return pltpu.make_async_copy(tile(h_hbm, src_smem[s, i]), tile(xbuf.at[s], i), gat_sem.at[s])

    def scatter_row(i, s):
        return pltpu.make_async_copy(tile(ybuf, i), tile(out_hbm, dst_smem[s, i]), sca_sem)

    def wait_gather(s):
        pltpu.make_async_copy(h_hbm.at[pl.ds(0, bm * SUBLANES)], xbuf.at[s], gat_sem.at[s]).wait()

    def wait_scatter():
        pltpu.make_async_copy(ybuf, out_hbm.at[pl.ds(0, bm * SUBLANES)], sca_sem).wait()

    def for_all_rows(start_row):
        def issue(i, c):
            start_row(i)
            return c
        lax.fori_loop(0, bm, issue, 0)

    @pl.when(f == 0)
    def _():
        @pl.when(b == 0)
        def _():
            src_copy(0, 0).start()
            src_copy(1, 1).start()
            dst_copy(0, 0).start()
            src_copy(0, 0).wait()
            for_all_rows(lambda i: gather_row(i, 0).start())
            ybuf[...] = jnp.zeros_like(ybuf)

        src_copy(b + 1, nxt).wait()
        dst_copy(b, slot).wait()
        wait_gather(slot)
        for s in range(SUBLANES):
            xb16[:, s * LANES:(s + 1) * LANES] = xbuf[slot, pl.ds(s, bm, stride=SUBLANES), :].astype(BF16)
        acc_ref[...] = jnp.zeros_like(acc_ref)

    base = f * group
    used = b < nused_ref[0]

    @pl.when(used)
    def _():
        for j in range(always):
            gather_row(base + j, nxt).start()
            scatter_row(base + j, slot).start()
        x = xb16[...]
        a = (_silu(_dot(x, wg_ref[0])) * _dot(x, wu_ref[0])).astype(BF16)
        acc_ref[...] += _dot(a, wd_ref[0])

    @pl.when(jnp.logical_not(used))
    def _():
        def issue(j, c):
            gather_row(base + j, nxt).start()
            scatter_row(base + j, slot).start()
            return c
        lax.fori_loop(0, always, issue, 0)

    @pl.when(f < nf - 1)
    def _():
        for j in range(always, group):
            gather_row(base + j, nxt).start()
            scatter_row(base + j, slot).start()

    @pl.when(f == nf - 1)
    def _():
        wait_scatter()
        for s in range(SUBLANES):
            ybuf[pl.ds(s, bm, stride=SUBLANES), :] = acc_ref[:, s * LANES:(s + 1) * LANES]
        dst_copy(b + 1, nxt).start()

        @pl.when(b + 2 <= nb)
        def _():
            src_copy(b + 2, slot).start()

        @pl.when(b == nb - 1)
        def _():
            dst_copy(b + 1, nxt).wait()
            for_all_rows(lambda i: scatter_row(i, nxt).start())
            wait_scatter()
            wait_gather(nxt)


def _moe_experts(block_e, nused, srcx, dstx, h, wg, wu, wd, n_rows_out):
    n_blocks = srcx.shape[0] - 1
    bm = srcx.shape[1]
    tf = FFN_COL_TILE
    assert D_FF // tf == MOE_ISSUE_STEPS
    grid_spec = pltpu.PrefetchScalarGridSpec(
        num_scalar_prefetch=2,
        grid=(n_blocks, D_FF // tf),
        in_specs=[pl.BlockSpec(memory_space=pl.ANY),
                  pl.BlockSpec(memory_space=pl.ANY),
                  pl.BlockSpec(memory_space=pl.ANY),
                  pl.BlockSpec((1, D_MODEL, tf), lambda b, f, be, nu: (be[b], 0, f)),
                  pl.BlockSpec((1, D_MODEL, tf), lambda b, f, be, nu: (be[b], 0, f)),
                  pl.BlockSpec((1, tf, D_MODEL), lambda b, f, be, nu: (be[b], f, 0))],
        out_specs=pl.BlockSpec(memory_space=pl.ANY),
        scratch_shapes=[pltpu.SMEM((2, bm), jnp.int32),
                        pltpu.SMEM((2, bm), jnp.int32),
                        pltpu.VMEM((2, bm * SUBLANES, LANES), F32),
                        pltpu.VMEM((bm, D_MODEL), BF16),
                        pltpu.VMEM((bm, D_MODEL), F32),
                        pltpu.VMEM((bm * SUBLANES, LANES), F32),
                        pltpu.SemaphoreType.DMA((2,)),
                        pltpu.SemaphoreType.DMA((2,)),
                        pltpu.SemaphoreType.DMA((2,)),
                        pltpu.SemaphoreType.DMA(())],
    )
    return pl.pallas_call(
        functools.partial(_moe_kernel, bm=bm),
        grid_spec=grid_spec,
        out_shape=jax.ShapeDtypeStruct((n_rows_out * SUBLANES, LANES), F32),
        compiler_params=_cparams("arbitrary", "arbitrary"),
        name="moe_experts",
    )(block_e, nused, srcx, dstx, h, wg, wu, wd)


def _combine_kernel(y1_ref, y2_ref, route_ref, xa_ref, xb_ref, g_ref, mod_ref, oa_ref, ob_ref, y_ref,
                    *, na):
    i = pl.program_id(0)
    r = route_ref[...]
    tm = r.shape[0]
    for s in range(SUBLANES):
        rows = pl.ds(s, tm, stride=SUBLANES)
        y_ref[:, s * LANES:(s + 1) * LANES] = r[:, 2:3] * y1_ref[rows, :] + r[:, 3:4] * y2_ref[rows, :]
    upd = mod_ref[0][5:6] * _rms_gain(y_ref[...], g_ref[...])

    @pl.when(i < na)
    def _():
        oa_ref[...] = xa_ref[...] + upd

    @pl.when(i >= na)
    def _():
        ob_ref[...] = xb_ref[...] + upd


def _moe_combine(y, route, xa, xb, gain, mod, La, Lb):
    tm = min(ROW_TILE, La, Lb)
    tt = _TwoTrunks(xa, xb, La, Lb, tm)
    nt = tt.T // tm
    return pl.pallas_call(
        functools.partial(_combine_kernel, na=tt.na),
        grid=(nt,),
        in_specs=[pl.BlockSpec((tm * SUBLANES, LANES), lambda i: (i, 0)),
                  pl.BlockSpec((tm * SUBLANES, LANES), lambda i: (nt + i, 0)),
                  pl.BlockSpec((tm, LANES), lambda i: (i, 0)),
                  pl.BlockSpec((tm, D_MODEL), tt.a_tile),
                  pl.BlockSpec((tm, D_MODEL), tt.b_tile),
                  pl.BlockSpec((1, D_MODEL), lambda i: (0, 0)),
                  pl.BlockSpec((1, 6, D_MODEL), tt.seq)],
        out_specs=[pl.BlockSpec((tm, D_MODEL), tt.a_tile),
                   pl.BlockSpec((tm, D_MODEL), tt.b_tile)],
        out_shape=[jax.ShapeDtypeStruct(xa.shape, F32), jax.ShapeDtypeStruct(xb.shape, F32)],
        scratch_shapes=[pltpu.VMEM((tm, D_MODEL), F32)],
        compiler_params=_cparams("arbitrary"),
        name="moe_combine_residual",
    )(y, y, route, xa, xb, gain, mod)


def _moe_plan(route, counts, T, bm):
    n_assign = TOP_K * T
    n_blocks = n_assign // bm + N_EXPERTS
    experts = jnp.arange(N_EXPERTS, dtype=jnp.int32)
    e_flat = jnp.concatenate([route[:, k] for k in range(TOP_K)]).astype(jnp.int32)
    _, order = lax.sort_key_val(e_flat, jnp.arange(n_assign, dtype=jnp.int32))
    counts = counts[0, :N_EXPERTS].astype(jnp.int32)
    padded = ((counts + bm - 1) // bm) * bm
    start = jnp.cumsum(counts) - counts
    pend = jnp.cumsum(padded)
    pstart = pend - padded
    pos = jnp.arange(n_blocks * bm, dtype=jnp.int32)
    pe = jnp.minimum(jnp.sum(pos[:, None] >= pend[None, :], axis=1), N_EXPERTS - 1).astype(jnp.int32)
    mine = pe[:, None] == experts[None, :]

    def of_expert(table):
        return jnp.sum(jnp.where(mine, table[None, :], 0), axis=1).astype(jnp.int32)

    rank = pos - of_expert(pstart)
    valid = rank < of_expert(counts)
    a = jnp.take(order, jnp.clip(of_expert(start) + rank, 0, n_assign - 1))
    spare = (n_assign + pos % bm).astype(jnp.int32)
    dst = jnp.where(valid, a, spare)
    src = jnp.where(valid, a % T, 0)
    block_e = pe[::bm]
    srcx = jnp.concatenate([src.reshape(n_blocks, bm), jnp.zeros((1, bm), jnp.int32)], axis=0)
    dstx = jnp.concatenate([spare[:bm].reshape(1, bm), dst.reshape(n_blocks, bm)], axis=0)
    nused = (pend[N_EXPERTS - 1] // bm).astype(jnp.int32).reshape(1)
    return block_e, nused, srcx, dstx


def _moe_layer(xa, xb, gain2, gain3, mod, wr, wg, wu, wd, La, Lb):
    T = xa.shape[0] + xb.shape[0]
    bm = min(MOE_BLOCK, T)
    h, route, counts = _router(xa, xb, gain2, mod, wr, La, Lb)
    block_e, nused, srcx, dstx = _moe_plan(route, counts, T, bm)
    y = _moe_experts(block_e, nused, srcx, dstx, h, wg, wu, wd, TOP_K * T + bm)
    return _moe_combine(y, route, xa, xb, gain3, mod, La, Lb)


def _prep_weights(gla_w_in, gla_w_gate_up, gla_b_gate, gla_norm, gla_w_out, na_w_qkv, na_rpb,
                  na_w_out, ffn_w_gate, ffn_w_up, ffn_w_down, moe_router, moe_w_gate, moe_w_up,
                  moe_w_down):
    n_a = gla_w_in.shape[0]
    n_b = na_w_qkv.shape[0]
    R = GLA_GATE_RANK
    n_qkvr = 2 * GLA_DK + 2 * GLA_DV
    prep = {'gla': [], 'na': [], 'ffn': [], 'moe': []}
    for j in range(n_a):
        w = gla_w_in[j]
        w1 = jnp.concatenate([w[:, n_qkvr:n_qkvr + R], w[:, n_qkvr + R:]], axis=0)
        w1 = jnp.pad(w1, ((0, 0), (0, LANES - R)))
        w2 = jnp.pad(gla_w_gate_up[j], ((0, 0), (0, LANES - R), (0, 0)))
        fold = _small_matmul(w1, w2, jnp.zeros((2, 1, GLA_DK), F32),
                             act=False, col_tile=GLA_DK)
        w_z = jnp.concatenate([fold[0, :D_MODEL], fold[1, D_MODEL:]], axis=1)
        w_q = w[:, :GLA_DK] * (GLA_DK_HEAD ** -0.5)
        w_all = jnp.concatenate([w_q, w[:, GLA_DK:n_qkvr], w_z], axis=1).astype(BF16)
        prep['gla'].append(dict(w_all=w_all, zb=gla_b_gate[j].reshape(1, 2 * GLA_DK),
                                norm=gla_norm[j].reshape(1, GLA_DV),
                                w_out=gla_w_out[j].astype(BF16)))
        prep['ffn'].append(dict(wg=ffn_w_gate[j].astype(BF16), wu=ffn_w_up[j].astype(BF16),
                                wd=ffn_w_down[j].astype(BF16)))
    for j in range(n_b):
        w = na_w_qkv[j]
        w_qkv = jnp.concatenate([w[:, :D_MODEL] * (NA_HEAD_DIM ** -0.5 * LOG2E), w[:, D_MODEL:]],
                                axis=1).astype(BF16)
        prep['na'].append(dict(w_qkv=w_qkv, bias=_na_bias_table(na_rpb[j]),
                               w_out=na_w_out[j].astype(BF16)))
        wr = jnp.pad(moe_router[j], ((0, 0), (0, LANES - N_EXPERTS)))
        wr_hi = wr.astype(BF16)
        wr = jnp.concatenate([wr_hi, (wr - wr_hi.astype(F32)).astype(BF16)], axis=1)
        prep['moe'].append(dict(wr=wr, wg=moe_w_gate[j].astype(BF16), wu=moe_w_up[j].astype(BF16),
                                wd=moe_w_down[j].astype(BF16)))
    return prep


def _trunks(xs, mod_all, norm_gains, prep):
    shapes = [x.shape for x in xs]
    xs = [x.reshape(-1, D_MODEL) for x in xs]
    Ba = shapes[0][0]
    for i in range(DEPTH):
        j = i // 2
        mods = [mod_all[i, :Ba], mod_all[i, Ba:]]
        gains = [norm_gains[i, n].reshape(1, D_MODEL) for n in range(4)]
        for t, (B, L, _) in enumerate(shapes):
            x, mod = xs[t], mods[t]
            if i % 2 == 0:
                p = prep['gla'][j]
                main, z = _proj(x, gains[0], mod, p['w_all'], p['zb'], L,
                                2 * GLA_DK + 2 * GLA_DV, 2 * GLA_DK)
                o = _gla_core(main, z, p['norm'], B, L)
                f = prep['ffn'][j]
                x = _ffn(o, p['w_out'], x, gains[1], gains[2], gains[3], mod,
                         f['wg'], f['wu'], f['wd'], L)
            else:
                p = prep['na'][j]
                qkv = _proj(x, gains[0], mod, p['w_qkv'], None, L, 3 * D_MODEL, 0)
                o = _na_core(qkv, p['bias'], B, L)
                x = _outproj(o, p['w_out'], x, gains[1], mod, L)
            xs[t] = x
        if i % 2 == 1:
            m = prep['moe'][j]
            xs = list(_moe_layer(xs[0], xs[1], gains[2], gains[3], mod_all[i], m['wr'], m['wg'],
                                 m['wu'], m['wd'], shapes[0][1], shapes[1][1]))
    return tuple(x.reshape(s) for x, s in zip(xs, shapes))


def kernel(x_prompt, x_sample, c_prompt, c_sample, ada_w, ada_b, norm_gains, gla_w_in, gla_w_gate_up, gla_b_gate, gla_norm, gla_w_out, na_w_qkv, na_rpb, na_w_out, ffn_w_gate, ffn_w_up, ffn_w_down, moe_router, moe_w_gate, moe_w_up, moe_w_down):
    prep = _prep_weights(gla_w_in, gla_w_gate_up, gla_b_gate, gla_norm, gla_w_out, na_w_qkv, na_rpb,
                         na_w_out, ffn_w_gate, ffn_w_up, ffn_w_down, moe_router, moe_w_gate,
                         moe_w_up, moe_w_down)
    bp, bs = c_prompt.shape[0], c_sample.shape[0]
    rows = -(-(bp + bs) // 8) * 8
    c = jnp.zeros((rows, D_MODEL), F32).at[:bp].set(c_prompt).at[bp:bp + bs].set(c_sample)
    mod = _small_matmul(c, ada_w, ada_b.reshape(DEPTH, 1, 6 * D_MODEL), act=True, col_tile=D_MODEL)
    mod = mod[:, :bp + bs].reshape(DEPTH, bp + bs, 6, D_MODEL)
    return _trunks((x_prompt, x_sample), mod, norm_gains, prep)
```

```python
import functools

import jax
import jax.numpy as jnp
from jax import lax
from jax.experimental import pallas as pl
from jax.experimental.pallas import tpu as pltpu

D_MODEL = 1024
DEPTH = 4
GRID_W = 64
GLA_HEADS = 4
GLA_DK = D_MODEL // 2
GLA_DV = D_MODEL
GLA_DK_HEAD = GLA_DK // GLA_HEADS
GLA_DV_HEAD = GLA_DV // GLA_HEADS
GLA_GATE_RANK = 16
GLA_TAU = 16.0
GLA_CHUNK = 64
NA_HEADS = 16
NA_HEAD_DIM = D_MODEL // NA_HEADS
WIN_ROWS = 8
WIN_COLS = 16
D_FF = 7 * D_MODEL // 2
N_EXPERTS = 8
TOP_K = 2
NORM_EPS = 1e-6
NEG_INF = -1e30
LOG2E = 1.4426950408889634

BF16 = jnp.bfloat16
F32 = jnp.float32

VMEM_LIMIT_BYTES = 56 * 1024 * 1024
LANES = 128
SUBLANES = 8

ROW_TILE = 512
FFN_ROW_TILE = 1024
FFN_COL_TILE = 512
GLA_TIME_BLOCK = 1024
NA_Q_ROWS = 4
NA_K_ROWS = 12
NA_TILES_PER_STEP = 4
MOE_BLOCK = 1024
MOE_ISSUE_STEPS = D_FF // FFN_COL_TILE
MOE_FIRST_STEP_ROWS = 240


def _cparams(*sem):
    return pltpu.CompilerParams(dimension_semantics=sem, vmem_limit_bytes=VMEM_LIMIT_BYTES)


def _norm_mod(x, gain, scale, shift):
    ms = jnp.mean(x * x, axis=-1, keepdims=True)
    y = x * lax.rsqrt(ms + NORM_EPS) * gain
    return y * (1.0 + scale) + shift


def _rms_gain(y, gain):
    ms = jnp.mean(y * y, axis=-1, keepdims=True)
    return y * lax.rsqrt(ms + NORM_EPS) * gain


def _dot(a, b):
    return jnp.dot(a, b, preferred_element_type=F32)


def _dot_nt(a, b):
    return lax.dot_general(a, b, (((1,), (1,)), ((), ())), preferred_element_type=F32)


def _dot_tn(a, b):
    return lax.dot_general(a, b, (((0,), (0,)), ((), ())), preferred_element_type=F32)


def _silu(x):
    return x * (1.0 / (1.0 + jnp.exp(-x)))


def _small_matmul_kernel(a_ref, w_ref, b_ref, o_ref, *, act):
    a = a_ref[...]
    if act:
        a = _silu(a)
    o_ref[0] = jnp.dot(a, w_ref[0], preferred_element_type=F32,
                       precision=lax.Precision.HIGHEST) + b_ref[0]


def _small_matmul(a, w, b, *, act, col_tile):
    M, K = a.shape
    G, _, N = w.shape
    return pl.pallas_call(
        functools.partial(_small_matmul_kernel, act=act),
        grid=(G, N // col_tile),
        in_specs=[pl.BlockSpec((M, K), lambda g, j: (0, 0)),
                  pl.BlockSpec((1, K, col_tile), lambda g, j: (g, 0, j)),
                  pl.BlockSpec((1, 1, col_tile), lambda g, j: (g, 0, j))],
        out_specs=pl.BlockSpec((1, M, col_tile), lambda g, j: (g, 0, j)),
        out_shape=jax.ShapeDtypeStruct((G, M, N), F32),
        compiler_params=_cparams("parallel", "parallel"),
        name="small_matmul",
    )(a, w, b)


def _proj_kernel(x_ref, g_ref, mod_ref, w_ref, zb_ref, o_ref, z_ref, *, n_main, n_z, cn):
    m = mod_ref[0]
    h = _norm_mod(x_ref[...], g_ref[...], m[1:2], m[0:1]).astype(BF16)
    for n in range(0, n_main, cn):
        o_ref[:, n:n + cn] = _dot(h, w_ref[:, n:n + cn]).astype(o_ref.dtype)
    for n in range(0, n_z, cn):
        z_ref[:, n:n + cn] = _dot(h, w_ref[:, n_main + n:n_main + n + cn]) + zb_ref[:, n:n + cn]


def _proj_kernel_noz(x_ref, g_ref, mod_ref, w_ref, o_ref, *, n_main, cn):
    m = mod_ref[0]
    h = _norm_mod(x_ref[...], g_ref[...], m[1:2], m[0:1]).astype(BF16)
    for n in range(0, n_main, cn):
        o_ref[:, n:n + cn] = _dot(h, w_ref[:, n:n + cn]).astype(o_ref.dtype)


def _proj(x, gain, mod, w, zb, L, n_main, n_z):
    T = x.shape[0]
    tm = min(ROW_TILE, L)
    spt = L // tm
    N = n_main + n_z
    in_specs = [pl.BlockSpec((tm, D_MODEL), lambda i: (i, 0)),
                pl.BlockSpec((1, D_MODEL), lambda i: (0, 0)),
                pl.BlockSpec((1, 6, D_MODEL), lambda i: (i // spt, 0, 0)),
                pl.BlockSpec((D_MODEL, N), lambda i: (0, 0))]
    if n_z:
        return pl.pallas_call(
            functools.partial(_proj_kernel, n_main=n_main, n_z=n_z, cn=512),
            grid=(T // tm,),
            in_specs=in_specs + [pl.BlockSpec((1, n_z), lambda i: (0, 0))],
            out_specs=[pl.BlockSpec((tm, n_main), lambda i: (i, 0)),
                       pl.BlockSpec((tm, n_z), lambda i: (i, 0))],
            out_shape=[jax.ShapeDtypeStruct((T, n_main), BF16),
                       jax.ShapeDtypeStruct((T, n_z), F32)],
            compiler_params=_cparams("parallel"),
            name="norm_proj_gla",
        )(x, gain, mod, w, zb)
    return pl.pallas_call(
        functools.partial(_proj_kernel_noz, n_main=n_main, cn=512),
        grid=(T // tm,),
        in_specs=in_specs,
        out_specs=pl.BlockSpec((tm, n_main), lambda i: (i, 0)),
        out_shape=jax.ShapeDtypeStruct((T, n_main), BF16),
        compiler_params=_cparams("parallel"),
        name="norm_proj_na",
    )(x, gain, mod, w)


def _outproj_kernel(o_ref, w_ref, x_ref, g_ref, mod_ref, y_ref):
    y = _dot(o_ref[...], w_ref[...])
    gate = mod_ref[0][2:3]
    y_ref[...] = x_ref[...] + gate * _rms_gain(y, g_ref[...])


def _outproj(o, w, x, gain, mod, L):
    T, K = o.shape
    tm = min(ROW_TILE, L)
    spt = L // tm
    return pl.pallas_call(
        _outproj_kernel,
        grid=(T // tm,),
        in_specs=[pl.BlockSpec((tm, K), lambda i: (i, 0)),
                  pl.BlockSpec((K, D_MODEL), lambda i: (0, 0)),
                  pl.BlockSpec((tm, D_MODEL), lambda i: (i, 0)),
                  pl.BlockSpec((1, D_MODEL), lambda i: (0, 0)),
                  pl.BlockSpec((1, 6, D_MODEL), lambda i: (i // spt, 0, 0))],
        out_specs=pl.BlockSpec((tm, D_MODEL), lambda i: (i, 0)),
        out_shape=jax.ShapeDtypeStruct((T, D_MODEL), F32),
        compiler_params=_cparams("parallel"),
        name="outproj_residual",
    )(o, w, x, gain, mod)


def _ffn_kernel(o_ref, wo_ref, x_ref, g1_ref, g2_ref, g3_ref, mod_ref, wg_ref, wu_ref, wd_ref,
                y_ref, h_ref, acc_ref):
    f = pl.program_id(1)
    m = mod_ref[0]

    @pl.when(f == 0)
    def _():
        x1 = x_ref[...] + m[2:3] * _rms_gain(_dot(o_ref[...], wo_ref[...]), g1_ref[...])
        y_ref[...] = x1
        h_ref[...] = _norm_mod(x1, g2_ref[...], m[4:5], m[3:4]).astype(BF16)
        acc_ref[...] = jnp.zeros_like(acc_ref)

    h = h_ref[...]
    a = (_silu(_dot(h, wg_ref[...])) * _dot(h, wu_ref[...])).astype(BF16)
    acc_ref[...] += _dot(a, wd_ref[...])

    @pl.when(f == pl.num_programs(1) - 1)
    def _():
        y_ref[...] = y_ref[...] + m[5:6] * _rms_gain(acc_ref[...], g3_ref[...])


def _ffn(o, wo, x, gain1, gain2, gain3, mod, wg, wu, wd, L):
    T = x.shape[0]
    tm = min(FFN_ROW_TILE, L)
    spt = L // tm
    tf = FFN_COL_TILE
    return pl.pallas_call(
        _ffn_kernel,
        grid=(T // tm, D_FF // tf),
        in_specs=[pl.BlockSpec((tm, D_MODEL), lambda i, f: (i, 0)),
                  pl.BlockSpec((D_MODEL, D_MODEL), lambda i, f: (0, 0)),
                  pl.BlockSpec((tm, D_MODEL), lambda i, f: (i, 0)),
                  pl.BlockSpec((1, D_MODEL), lambda i, f: (0, 0)),
                  pl.BlockSpec((1, D_MODEL), lambda i, f: (0, 0)),
                  pl.BlockSpec((1, D_MODEL), lambda i, f: (0, 0)),
                  pl.BlockSpec((1, 6, D_MODEL), lambda i, f: (i // spt, 0, 0)),
                  pl.BlockSpec((D_MODEL, tf), lambda i, f: (0, f)),
                  pl.BlockSpec((D_MODEL, tf), lambda i, f: (0, f)),
                  pl.BlockSpec((tf, D_MODEL), lambda i, f: (f, 0))],
        out_specs=pl.BlockSpec((tm, D_MODEL), lambda i, f: (i, 0)),
        out_shape=jax.ShapeDtypeStruct((T, D_MODEL), F32),
        scratch_shapes=[pltpu.VMEM((tm, D_MODEL), BF16), pltpu.VMEM((tm, D_MODEL), F32)],
        compiler_params=_cparams("parallel", "arbitrary"),
        name="ffn_swiglu",
    )(o, wo, x, gain1, gain2, gain3, mod, wg, wu, wd)


def _gla_block(q_ref, k_ref, v_ref, z_ref, S, *, tb, reverse):
    C = GLA_CHUNK
    chunks = list(range(tb // C))
    order = chunks[::-1] if reverse else chunks
    sl = {c: slice(c * C, (c + 1) * C) for c in chunks}
    row = lax.broadcasted_iota(jnp.int32, (C, C), 0)
    col = lax.broadcasted_iota(jnp.int32, (C, C), 1)
    if reverse:
        tri = col >= row
        allowed = col > row
        i_ref, i_last = C - 1 - C // 2, 0
    else:
        tri = row >= col
        allowed = row >= col
        i_ref, i_last = C // 2, C - 1
    tri = jnp.where(tri, 1.0, 0.0).astype(BF16)
    dk = q_ref.shape[1]
    reps = v_ref.shape[1] // dk

    z = z_ref[...]
    g = (jnp.minimum(z, 0.0) - jnp.log(1.0 + jnp.exp(-jnp.abs(z)))) * (1.0 / GLA_TAU)
    g_hi = g.astype(BF16)
    rest = g - g_hi.astype(F32)
    g_mid = rest.astype(BF16)
    g_lo = (rest - g_mid.astype(F32)).astype(BF16)
    g_terms = jnp.concatenate([g_hi, g_mid, g_lo], axis=1)
    G = {}
    for c in chunks:
        t = _dot(tri, g_terms[sl[c]])
        G[c] = t[:, :dk] + t[:, dk:2 * dk] + t[:, 2 * dk:]
    q_rel, k_rel, q_in, k_out, decay = {}, {}, {}, {}, {}
    for c in chunks:
        q = q_ref[sl[c], :].astype(F32)
        k = k_ref[sl[c], :].astype(F32)
        G_ref = G[c][i_ref:i_ref + 1]
        G_last = G[c][i_last:i_last + 1]
        q_rel[c] = (q * jnp.exp(G[c] - G_ref)).astype(BF16)
        k_rel[c] = (k * jnp.exp(G_ref - G[c])).astype(BF16)
        q_in[c] = (q * jnp.exp(G[c])).astype(BF16)
        k_out[c] = (k * jnp.exp(G_last - G[c])).astype(BF16)
        d = jnp.broadcast_to(jnp.exp(G_last), (dk, dk)).T
        decay[c] = jnp.concatenate([d] * reps, axis=1)
    scores = {c: jnp.where(allowed, _dot_nt(q_rel[c], k_rel[c]), 0.0).astype(BF16) for c in chunks}
    kv = {c: _dot_tn(k_out[c], v_ref[sl[c], :]) for c in chunks}
    S_in = {}
    for c in order:
        S_in[c] = S.astype(BF16)
        S = decay[c] * S + kv[c]
    o = {c: _dot(scores[c], v_ref[sl[c], :]) + _dot(q_in[c], S_in[c]) for c in chunks}
    return o, S


def _gla_fwd_kernel(q_ref, k_ref, v_ref, z_ref, o_ref, s_ref, *, tb):
    @pl.when(pl.program_id(2) == 0)
    def _():
        s_ref[...] = jnp.zeros_like(s_ref)

    o, S = _gla_block(q_ref, k_ref, v_ref, z_ref, s_ref[...], tb=tb, reverse=False)
    for c, oc in o.items():
        o_ref[c * GLA_CHUNK:(c + 1) * GLA_CHUNK, :] = oc
    s_ref[...] = S


def _gla_bwd_kernel(q_ref, k_ref, v_ref, z_ref, of_ref, r_ref, gain_ref, o_ref, s_ref, *, tb):
    @pl.when(pl.program_id(2) == 0)
    def _():
        s_ref[...] = jnp.zeros_like(s_ref)

    o, S = _gla_block(q_ref, k_ref, v_ref, z_ref, s_ref[...], tb=tb, reverse=True)
    for c, oc in o.items():
        sl = slice(c * GLA_CHUNK, (c + 1) * GLA_CHUNK)
        oc = _rms_gain(oc + of_ref[sl, :], gain_ref[...])
        o_ref[sl, :] = (oc * _silu(r_ref[sl, :].astype(F32))).astype(o_ref.dtype)
    s_ref[...] = S


def _gla_core(main, z, gla_norm, B, L):
    T = B * L
    tb = min(GLA_TIME_BLOCK, L)
    nt = L // tb
    H = GLA_HEADS
    dk, dv = GLA_DK_HEAD, GLA_DV_HEAD
    grid = (B, H, nt)
    scratch = [pltpu.VMEM((dk, dv), F32)]

    def fwd_t(b, h, t):
        return b * nt + t

    def bwd_t(b, h, t):
        return b * nt + (nt - 1 - t)

    o_fwd = pl.pallas_call(
        functools.partial(_gla_fwd_kernel, tb=tb),
        grid=grid,
        in_specs=[pl.BlockSpec((tb, dk), lambda b, h, t: (fwd_t(b, h, t), h)),
                  pl.BlockSpec((tb, dk), lambda b, h, t: (fwd_t(b, h, t), H + h)),
                  pl.BlockSpec((tb, dv), lambda b, h, t: (fwd_t(b, h, t), H + h)),
                  pl.BlockSpec((tb, dk), lambda b, h, t: (fwd_t(b, h, t), h))],
        out_specs=pl.BlockSpec((tb, dv), lambda b, h, t: (fwd_t(b, h, t), h)),
        out_shape=jax.ShapeDtypeStruct((T, GLA_DV), F32),
        scratch_shapes=scratch,
        compiler_params=_cparams("parallel", "parallel", "arbitrary"),
        name="gla_forward",
    )(main, main, main, z)
    return pl.pallas_call(
        functools.partial(_gla_bwd_kernel, tb=tb),
        grid=grid,
        in_specs=[pl.BlockSpec((tb, dk), lambda b, h, t: (bwd_t(b, h, t), h)),
                  pl.BlockSpec((tb, dk), lambda b, h, t: (bwd_t(b, h, t), H + h)),
                  pl.BlockSpec((tb, dv), lambda b, h, t: (bwd_t(b, h, t), H + h)),
                  pl.BlockSpec((tb, dk), lambda b, h, t: (bwd_t(b, h, t), H + h)),
                  pl.BlockSpec((tb, dv), lambda b, h, t: (bwd_t(b, h, t), h)),
                  pl.BlockSpec((tb, dv), lambda b, h, t: (bwd_t(b, h, t), 2 * H + h)),
                  pl.BlockSpec((1, dv), lambda b, h, t: (0, h))],
        out_specs=pl.BlockSpec((tb, dv), lambda b, h, t: (bwd_t(b, h, t), h)),
        out_shape=jax.ShapeDtypeStruct((T, GLA_DV), BF16),
        scratch_shapes=scratch,
        compiler_params=_cparams("parallel", "parallel", "arbitrary"),
        name="gla_backward_finish",
    )(main, main, main, z, o_fwd, main, gla_norm)


def _na_bias_table(rpb):
    cols = jnp.arange(GRID_W)
    col_start = jnp.clip(cols - WIN_COLS // 2, 0, GRID_W - WIN_COLS)
    in_win = (cols[None, :] >= col_start[:, None]) & (cols[None, :] < col_start[:, None] + WIN_COLS)
    dc_idx = jnp.clip(cols[None, :] - cols[:, None] + WIN_COLS - 1, 0, 2 * WIN_COLS - 2)
    pick_c = (dc_idx[:, :, None] == jnp.arange(2 * WIN_COLS - 1)).astype(F32)
    a = jnp.arange(NA_Q_ROWS)
    j = jnp.arange(NA_K_ROWS)
    tables = []
    for delta, win_start in ((0, jnp.zeros_like(a)), (4, a), (8, jnp.full_like(a, 4))):
        live_row = (j[None, :] >= win_start[:, None]) & (j[None, :] < win_start[:, None] + WIN_ROWS)
        dr_idx = jnp.clip(j[None, :] - delta - a[:, None] + WIN_ROWS - 1, 0, 2 * WIN_ROWS - 2)
        pick_r = (dr_idx[:, :, None] == jnp.arange(2 * WIN_ROWS - 1)).astype(F32)
        bias = jnp.einsum('hrc,ajr,qkc->haqjk', rpb.astype(F32), pick_r, pick_c,
                          precision=lax.Precision.HIGHEST) * LOG2E
        live = live_row[:, None, :, None] & in_win[None, :, None, :]
        tables.append(jnp.where(live[None], bias, NEG_INF))
    t = jnp.stack(tables)
    return t.reshape(3, NA_HEADS, NA_Q_ROWS * GRID_W, NA_K_ROWS * GRID_W)


def _na_kernel(q_ref, k_ref, v_ref, bias_ref, o_ref, *, rows):
    ntile = rows // NA_Q_ROWS
    tq = NA_Q_ROWS * GRID_W
    tk = NA_K_ROWS * GRID_W
    lane = lax.broadcasted_iota(jnp.int32, (1, LANES), 1)
    first_head = lane < NA_HEAD_DIM

    def body(tt, carry):
        q0, geo, k, v, s = {}, {}, {}, {}, {}
        units = [(u, hh) for u in range(NA_TILES_PER_STEP) for hh in range(2)]
        for u in range(NA_TILES_PER_STEP):
            t = NA_TILES_PER_STEP * tt + u
            kr0 = jnp.clip(NA_Q_ROWS * t - 4, 0, rows - NA_K_ROWS)
            geo[u] = jnp.where(t == 0, 0, jnp.where(t == ntile - 1, 2, 1))
            q0[u] = pl.multiple_of(t * tq, tq)
            k0 = pl.multiple_of(kr0 * GRID_W, GRID_W)
            q = q_ref[pl.ds(q0[u], tq), :]
            k[u] = k_ref[pl.ds(k0, tk), :]
            v[u] = v_ref[pl.ds(k0, tk), :]
            for hh in range(2):
                sel = first_head if hh == 0 else jnp.logical_not(first_head)
                s[u, hh] = _dot_nt(jnp.where(sel, q, jnp.zeros_like(q)), k[u])
        p, inv_l, o = {}, {}, {}
        for u, hh in units:
            sb = s[u, hh] + bias_ref[geo[u], hh]
            e = jnp.exp2(sb - jnp.max(sb, axis=-1, keepdims=True))
            inv_l[u, hh] = 1.0 / jnp.sum(e, axis=-1, keepdims=True)
            p[u, hh] = e.astype(BF16)
        for u, hh in units:
            o[u, hh] = _dot(p[u, hh], v[u]) * inv_l[u, hh]
        for u in range(NA_TILES_PER_STEP):
            o_ref[pl.ds(q0[u], tq), :] = jnp.where(first_head, o[u, 0], o[u, 1]).astype(o_ref.dtype)
        return carry

    lax.fori_loop(0, ntile // NA_TILES_PER_STEP, body, 0)


def _na_core(qkv, bias, B, L):
    T = B * L
    rows = L // GRID_W
    npair = NA_HEADS // 2
    tq = NA_Q_ROWS * GRID_W
    tk = NA_K_ROWS * GRID_W
    assert (rows // NA_Q_ROWS) % NA_TILES_PER_STEP == 0 and rows >= NA_K_ROWS
    return pl.pallas_call(
        functools.partial(_na_kernel, rows=rows),
        grid=(npair, B),
        in_specs=[pl.BlockSpec((L, LANES), lambda p, b: (b, p)),
                  pl.BlockSpec((L, LANES), lambda p, b: (b, npair + p)),
                  pl.BlockSpec((L, LANES), lambda p, b: (b, 2 * npair + p)),
                  pl.BlockSpec((3, 2, tq, tk), lambda p, b: (0, p, 0, 0))],
        out_specs=pl.BlockSpec((L, LANES), lambda p, b: (b, p)),
        out_shape=jax.ShapeDtypeStruct((T, D_MODEL), BF16),
        compiler_params=_cparams("parallel", "parallel"),
        name="neighbourhood_attention",
    )(qkv, qkv, qkv, bias)


def _router_kernel(xa_ref, xb_ref, g_ref, mod_ref, wr_ref, h_ref, route_ref, cnt_ref, carry_ref, *,
                   na):
    i = pl.program_id(0)

    @pl.when(i == 0)
    def _():
        carry_ref[...] = jnp.zeros_like(carry_ref)

    m = mod_ref[0]
    x = jnp.where(i < na, xa_ref[...], xb_ref[...])
    h = _norm_mod(x, g_ref[...], m[4:5], m[3:4])
    tm = h.shape[0]
    for s in range(SUBLANES):
        h_ref[pl.ds(s, tm, stride=SUBLANES), :] = h[:, s * LANES:(s + 1) * LANES]
    h_hi = h.astype(BF16)
    h_lo = (h - h_hi.astype(F32)).astype(BF16)
    both = _dot(h_hi, wr_ref[...])
    logits = both[:, :LANES] + both[:, LANES:] + _dot(h_lo, wr_ref[:, :LANES])
    lane = lax.broadcasted_iota(jnp.int32, logits.shape, 1)
    valid = lane < N_EXPERTS
    l1 = jnp.where(valid, logits, -jnp.inf)
    m1 = jnp.max(l1, axis=-1, keepdims=True)
    i1 = jnp.min(jnp.where(l1 == m1, lane, LANES), axis=-1, keepdims=True)
    l2 = jnp.where(lane == i1, -jnp.inf, l1)
    m2 = jnp.max(l2, axis=-1, keepdims=True)
    i2 = jnp.min(jnp.where(l2 == m2, lane, LANES), axis=-1, keepdims=True)
    e = jnp.exp(m2 - m1)
    g1 = 1.0 / (1.0 + e)
    g2 = e * g1
    chosen = (lane == i1) | (lane == i2)
    carry_ref[...] = carry_ref[...] + jnp.sum(jnp.where(chosen, 1.0, 0.0), axis=0, keepdims=True)
    cnt_ref[...] = carry_ref[...]
    route = jnp.where(lane == 0, i1.astype(F32),
                      jnp.where(lane == 1, i2.astype(F32),
                                jnp.where(lane == 2, g1, jnp.where(lane == 3, g2, 0.0))))
    route_ref[...] = route


class _TwoTrunks:
    def __init__(self, xa, xb, La, Lb, tm):
        self.na = xa.shape[0] // tm
        self.nb = xb.shape[0] // tm
        self.T = xa.shape[0] + xb.shape[0]
        self.tm = tm
        seqs_a = xa.shape[0] // La
        tiles_per_seq_a = La // tm
        tiles_per_seq_b = Lb // tm
        na = self.na
        self.a_tile = lambda i: (jnp.minimum(i, na - 1), 0)
        self.b_tile = lambda i: (jnp.maximum(i - na, 0), 0)
        self.seq = lambda i: (jnp.where(i < na, i // tiles_per_seq_a,
                                        seqs_a + (i - na) // tiles_per_seq_b), 0, 0)


def _router(xa, xb, gain, mod, wr, La, Lb):
    tm = min(ROW_TILE, La, Lb)
    tt = _TwoTrunks(xa, xb, La, Lb, tm)
    T = tt.T
    return pl.pallas_call(
        functools.partial(_router_kernel, na=tt.na),
        grid=(T // tm,),
        in_specs=[pl.BlockSpec((tm, D_MODEL), tt.a_tile),
                  pl.BlockSpec((tm, D_MODEL), tt.b_tile),
                  pl.BlockSpec((1, D_MODEL), lambda i: (0, 0)),
                  pl.BlockSpec((1, 6, D_MODEL), tt.seq),
                  pl.BlockSpec((D_MODEL, 2 * LANES), lambda i: (0, 0))],
        out_specs=[pl.BlockSpec((tm * SUBLANES, LANES), lambda i: (i, 0)),
                   pl.BlockSpec((tm, LANES), lambda i: (i, 0)),
                   pl.BlockSpec((SUBLANES, LANES), lambda i: (0, 0))],
        out_shape=[jax.ShapeDtypeStruct((T * SUBLANES, LANES), F32),
                   jax.ShapeDtypeStruct((T, LANES), F32),
                   jax.ShapeDtypeStruct((SUBLANES, LANES), F32)],
        scratch_shapes=[pltpu.VMEM((SUBLANES, LANES), F32)],
        compiler_params=_cparams("arbitrary"),
        name="moe_router",
    )(xa, xb, gain, mod, wr)


def _moe_kernel(be_ref, nused_ref, srcx_hbm, dstx_hbm, h_hbm, wg_ref, wu_ref, wd_ref, out_hbm,
                src_smem, dst_smem, xbuf, xb16, acc_ref, ybuf,
                src_sem, dst_sem, gat_sem, sca_sem, order_sem, *, bm):
    b = pl.program_id(0)
    f = pl.program_id(1)
    nb = pl.num_programs(0)
    nf = pl.num_programs(1)
    slot = b % 2
    nxt = 1 - slot
    first = MOE_FIRST_STEP_ROWS
    group = (bm - first) // MOE_ISSUE_STEPS
    assert first % SUBLANES == 0 and first + group * MOE_ISSUE_STEPS == bm

    def src_copy(r, s):
        return pltpu.make_async_copy(srcx_hbm.at[r], src_smem.at[s], src_sem.at[s])

    def dst_copy(r, s):
        return pltpu.make_async_copy(dstx_hbm.at[r], dst_smem.at[s], dst_sem.at[s])

    def tile(ref, i):
        return ref.at[pl.ds(pl.multiple_of(i * SUBLANES, SUBLANES), SUBLANES)]

    def gather_row(i, s):
        return pltpu.make_async_copy(tile(h_hbm, src_smem[s, i]), tile(xbuf.at[s], i), gat_sem.at[s])

    def scatter_row(i, s):
        return pltpu.make_async_copy(tile(ybuf, i), tile(out_hbm, dst_smem[s, i]), sca_sem)

    def wait_gather(s):
        pltpu.make_async_copy(h_hbm.at[pl.ds(0, bm * SUBLANES)], xbuf.at[s], gat_sem.at[s]).wait()

    def wait_scatter():
        pltpu.make_async_copy(ybuf, out_hbm.at[pl.ds(0, bm * SUBLANES)], sca_sem).wait()

    def for_all_rows(start_row):
        def issue(i, c):
            start_row(i)
            return c
        lax.fori_loop(0, bm, issue, 0)

    def issue_rows(row0, lo, hi):
        for j in range(lo, hi):
            gather_row(row0 + j, nxt).start()
            scatter_row(row0 + j, slot).start()
        return jnp.minimum(pl.semaphore_read(order_sem), 0)

    @pl.when(f == 0)
    def _():
        @pl.when(b == 0)
        def _():
            src_copy(0, 0).start()
            src_copy(1, 1).start()
            dst_copy(0, 0).start()
            src_copy(0, 0).wait()
            for_all_rows(lambda i: gather_row(i, 0).start())
            ybuf[...] = jnp.zeros_like(ybuf)

        src_copy(b + 1, nxt).wait()
        dst_copy(b, slot).wait()
        wait_gather(slot)
        per = first // SUBLANES
        for s in range(SUBLANES):
            zero = issue_rows(0, s * per, (s + 1) * per)
            xb16[:, s * LANES:(s + 1) * LANES] = (
                xbuf[slot + zero, pl.ds(s, bm, stride=SUBLANES), :].astype(BF16))
        acc_ref[...] = jnp.zeros_like(acc_ref)

    base = first + f * group
    used = b < nused_ref[0]

    @pl.when(used)
    def _():
        cut = [(group * n) // 3 for n in range(4)]
        x = xb16[...]
        g = _dot(x, wg_ref[0])
        zero = pl.multiple_of(issue_rows(base, cut[0], cut[1]), 16)
        u = _dot(x, wu_ref[0, pl.ds(zero, D_MODEL), :])
        zero = pl.multiple_of(issue_rows(base, cut[1], cut[2]), 16)
        a = (_silu(g) * u).astype(BF16)
        acc_ref[...] += _dot(a, wd_ref[0, pl.ds(zero, a.shape[1]), :])
        issue_rows(base, cut[2], cut[3])

    @pl.when(jnp.logical_not(used))
    def _():
        def issue(j, c):
            gather_row(base + j, nxt).start()
            scatter_row(base + j, slot).start()
            return c
        lax.fori_loop(0, group, issue, 0)

    @pl.when(f == nf - 1)
    def _():
        wait_scatter()
        for s in range(SUBLANES):
            ybuf[pl.ds(s, bm, stride=SUBLANES), :] = acc_ref[:, s * LANES:(s + 1) * LANES]
        dst_copy(b + 1, nxt).start()

        @pl.when(b + 2 <= nb)
        def _():
            src_copy(b + 2, slot).start()

        @pl.when(b == nb - 1)
        def _():
            dst_copy(b + 1, nxt).wait()
            for_all_rows(lambda i: scatter_row(i, nxt).start())
            wait_scatter()
            wait_gather(nxt)


def _moe_experts(block_e, nused, srcx, dstx, h, wg, wu, wd, n_rows_out):
    n_blocks = srcx.shape[0] - 1
    bm = srcx.shape[1]
    tf = FFN_COL_TILE
    assert D_FF // tf == MOE_ISSUE_STEPS
    grid_spec = pltpu.PrefetchScalarGridSpec(
        num_scalar_prefetch=2,
        grid=(n_blocks, D_FF // tf),
        in_specs=[pl.BlockSpec(memory_space=pl.ANY),
                  pl.BlockSpec(memory_space=pl.ANY),
                  pl.BlockSpec(memory_space=pl.ANY),
                  pl.BlockSpec((1, D_MODEL, tf), lambda b, f, be, nu: (be[b], 0, f)),
                  pl.BlockSpec((1, D_MODEL, tf), lambda b, f, be, nu: (be[b], 0, f)),
                  pl.BlockSpec((1, tf, D_MODEL), lambda b, f, be, nu: (be[b], f, 0))],
        out_specs=pl.BlockSpec(memory_space=pl.ANY),
        scratch_shapes=[pltpu.SMEM((2, bm), jnp.int32),
                        pltpu.SMEM((2, bm), jnp.int32),
                        pltpu.VMEM((2, bm * SUBLANES, LANES), F32),
                        pltpu.VMEM((bm, D_MODEL), BF16),
                        pltpu.VMEM((bm, D_MODEL), F32),
                        pltpu.VMEM((bm * SUBLANES, LANES), F32),
                        pltpu.SemaphoreType.DMA((2,)),
                        pltpu.SemaphoreType.DMA((2,)),
                        pltpu.SemaphoreType.DMA((2,)),
                        pltpu.SemaphoreType.DMA(()),
                        pltpu.SemaphoreType.REGULAR],
    )
    return pl.pallas_call(
        functools.partial(_moe_kernel, bm=bm),
        grid_spec=grid_spec,
        out_shape=jax.ShapeDtypeStruct((n_rows_out * SUBLANES, LANES), F32),
        compiler_params=_cparams("arbitrary", "arbitrary"),
        name="moe_experts",
    )(block_e, nused, srcx, dstx, h, wg, wu, wd)


def _combine_kernel(y1_ref, y2_ref, route_ref, xa_ref, xb_ref, g_ref, mod_ref, oa_ref, ob_ref, y_ref,
                    *, na):
    i = pl.program_id(0)
    r = route_ref[...]
    tm = r.shape[0]
    for s in range(SUBLANES):
        rows = pl.ds(s, tm, stride=SUBLANES)
        y_ref[:, s * LANES:(s + 1) * LANES] = r[:, 2:3] * y1_ref[rows, :] + r[:, 3:4] * y2_ref[rows, :]
    upd = mod_ref[0][5:6] * _rms_gain(y_ref[...], g_ref[...])

    @pl.when(i < na)
    def _():
        oa_ref[...] = xa_ref[...] + upd

    @pl.when(i >= na)
    def _():
        ob_ref[...] = xb_ref[...] + upd


def _moe_combine(y, route, xa, xb, gain, mod, La, Lb):
    tm = min(ROW_TILE, La, Lb)
    tt = _TwoTrunks(xa, xb, La, Lb, tm)
    nt = tt.T // tm
    return pl.pallas_call(
        functools.partial(_combine_kernel, na=tt.na),
        grid=(nt,),
        in_specs=[pl.BlockSpec((tm * SUBLANES, LANES), lambda i: (i, 0)),
                  pl.BlockSpec((tm * SUBLANES, LANES), lambda i: (nt + i, 0)),
                  pl.BlockSpec((tm, LANES), lambda i: (i, 0)),
                  pl.BlockSpec((tm, D_MODEL), tt.a_tile),
                  pl.BlockSpec((tm, D_MODEL), tt.b_tile),
                  pl.BlockSpec((1, D_MODEL), lambda i: (0, 0)),
                  pl.BlockSpec((1, 6, D_MODEL), tt.seq)],
        out_specs=[pl.BlockSpec((tm, D_MODEL), tt.a_tile),
                   pl.BlockSpec((tm, D_MODEL), tt.b_tile)],
        out_shape=[jax.ShapeDtypeStruct(xa.shape, F32), jax.ShapeDtypeStruct(xb.shape, F32)],
        scratch_shapes=[pltpu.VMEM((tm, D_MODEL), F32)],
        compiler_params=_cparams("arbitrary"),
        name="moe_combine_residual",
    )(y, y, route, xa, xb, gain, mod)


def _moe_plan(route, counts, T, bm):
    n_assign = TOP_K * T
    n_blocks = n_assign // bm + N_EXPERTS
    experts = jnp.arange(N_EXPERTS, dtype=jnp.int32)
    e_flat = jnp.concatenate([route[:, k] for k in range(TOP_K)]).astype(jnp.int32)
    _, order = lax.sort_key_val(e_flat, jnp.arange(n_assign, dtype=jnp.int32))
    counts = counts[0, :N_EXPERTS].astype(jnp.int32)
    padded = ((counts + bm - 1) // bm) * bm
    start = jnp.cumsum(counts) - counts
    pend = jnp.cumsum(padded)
    pstart = pend - padded
    pos = jnp.arange(n_blocks * bm, dtype=jnp.int32)
    pe = jnp.minimum(jnp.sum(pos[:, None] >= pend[None, :], axis=1), N_EXPERTS - 1).astype(jnp.int32)
    mine = pe[:, None] == experts[None, :]

    def of_expert(table):
        return jnp.sum(jnp.where(mine, table[None, :], 0), axis=1).astype(jnp.int32)

    rank = pos - of_expert(pstart)
    valid = rank < of_expert(counts)
    a = jnp.take(order, jnp.clip(of_expert(start) + rank, 0, n_assign - 1))
    spare = (n_assign + pos % bm).astype(jnp.int32)
    dst = jnp.where(valid, a, spare)
    src = jnp.where(valid, a % T, 0)
    block_e = pe[::bm]
    srcx = jnp.concatenate([src.reshape(n_blocks, bm), jnp.zeros((1, bm), jnp.int32)], axis=0)
    dstx = jnp.concatenate([spare[:bm].reshape(1, bm), dst.reshape(n_blocks, bm)], axis=0)
    nused = (pend[N_EXPERTS - 1] // bm).astype(jnp.int32).reshape(1)
    return block_e, nused, srcx, dstx


def _moe_layer(xa, xb, gain2, gain3, mod, wr, wg, wu, wd, La, Lb):
    T = xa.shape[0] + xb.shape[0]
    bm = min(MOE_BLOCK, T)
    h, route, counts = _router(xa, xb, gain2, mod, wr, La, Lb)
    block_e, nused, srcx, dstx = _moe_plan(route, counts, T, bm)
    y = _moe_experts(block_e, nused, srcx, dstx, h, wg, wu, wd, TOP_K * T + bm)
    return _moe_combine(y, route, xa, xb, gain3, mod, La, Lb)


def _prep_weights(gla_w_in, gla_w_gate_up, gla_b_gate, gla_norm, gla_w_out, na_w_qkv, na_rpb,
                  na_w_out, ffn_w_gate, ffn_w_up, ffn_w_down, moe_router, moe_w_gate, moe_w_up,
                  moe_w_down):
    n_a = gla_w_in.shape[0]
    n_b = na_w_qkv.shape[0]
    R = GLA_GATE_RANK
    n_qkvr = 2 * GLA_DK + 2 * GLA_DV
    prep = {'gla': [], 'na': [], 'ffn': [], 'moe': []}
    for j in range(n_a):
        w = gla_w_in[j]
        w1 = jnp.concatenate([w[:, n_qkvr:n_qkvr + R], w[:, n_qkvr + R:]], axis=0)
        w1 = jnp.pad(w1, ((0, 0), (0, LANES - R)))
        w2 = jnp.pad(gla_w_gate_up[j], ((0, 0), (0, LANES - R), (0, 0)))
        fold = _small_matmul(w1, w2, jnp.zeros((2, 1, GLA_DK), F32),
                             act=False, col_tile=GLA_DK)
        w_z = jnp.concatenate([fold[0, :D_MODEL], fold[1, D_MODEL:]], axis=1)
        w_q = w[:, :GLA_DK] * (GLA_DK_HEAD ** -0.5)
        w_all = jnp.concatenate([w_q, w[:, GLA_DK:n_qkvr], w_z], axis=1).astype(BF16)
        prep['gla'].append(dict(w_all=w_all, zb=gla_b_gate[j].reshape(1, 2 * GLA_DK),
                                norm=gla_norm[j].reshape(1, GLA_DV),
                                w_out=gla_w_out[j].astype(BF16)))
        prep['ffn'].append(dict(wg=ffn_w_gate[j].astype(BF16), wu=ffn_w_up[j].astype(BF16),
                                wd=ffn_w_down[j].astype(BF16)))
    for j in range(n_b):
        w = na_w_qkv[j]
        w_qkv = jnp.concatenate([w[:, :D_MODEL] * (NA_HEAD_DIM ** -0.5 * LOG2E), w[:, D_MODEL:]],
                                axis=1).astype(BF16)
        prep['na'].append(dict(w_qkv=w_qkv, bias=_na_bias_table(na_rpb[j]),
                               w_out=na_w_out[j].astype(BF16)))
        wr = jnp.pad(moe_router[j], ((0, 0), (0, LANES - N_EXPERTS)))
        wr_hi = wr.astype(BF16)
        wr = jnp.concatenate([wr_hi, (wr - wr_hi.astype(F32)).astype(BF16)], axis=1)
        prep['moe'].append(dict(wr=wr, wg=moe_w_gate[j].astype(BF16), wu=moe_w_up[j].astype(BF16),
                                wd=moe_w_down[j].astype(BF16)))
    return prep


def _trunks(xs, mod_all, norm_gains, prep):
    shapes = [x.shape for x in xs]
    xs = [x.reshape(-1, D_MODEL) for x in xs]
    Ba = shapes[0][0]
    for i in range(DEPTH):
        j = i // 2
        mods = [mod_all[i, :Ba], mod_all[i, Ba:]]
        gains = [norm_gains[i, n].reshape(1, D_MODEL) for n in range(4)]
        for t, (B, L, _) in enumerate(shapes):
            x, mod = xs[t], mods[t]
            if i % 2 == 0:
                p = prep['gla'][j]
                main, z = _proj(x, gains[0], mod, p['w_all'], p['zb'], L,
                                2 * GLA_DK + 2 * GLA_DV, 2 * GLA_DK)
                o = _gla_core(main, z, p['norm'], B, L)
                f = prep['ffn'][j]
                x = _ffn(o, p['w_out'], x, gains[1], gains[2], gains[3], mod,
                         f['wg'], f['wu'], f['wd'], L)
            else:
                p = prep['na'][j]
                qkv = _proj(x, gains[0], mod, p['w_qkv'], None, L, 3 * D_MODEL, 0)
                o = _na_core(qkv, p['bias'], B, L)
                x = _outproj(o, p['w_out'], x, gains[1], mod, L)
            xs[t] = x
        if i % 2 == 1:
            m = prep['moe'][j]
            xs = list(_moe_layer(xs[0], xs[1], gains[2], gains[3], mod_all[i], m['wr'], m['wg'],
                                 m['wu'], m['wd'], shapes[0][1], shapes[1][1]))
    return tuple(x.reshape(s) for x, s in zip(xs, shapes))


def kernel(x_prompt, x_sample, c_prompt, c_sample, ada_w, ada_b, norm_gains, gla_w_in, gla_w_gate_up, gla_b_gate, gla_norm, gla_w_out, na_w_qkv, na_rpb, na_w_out, ffn_w_gate, ffn_w_up, ffn_w_down, moe_router, moe_w_gate, moe_w_up, moe_w_down):
    prep = _prep_weights(gla_w_in, gla_w_gate_up, gla_b_gate, gla_norm, gla_w_out, na_w_qkv, na_rpb,
                         na_w_out, ffn_w_gate, ffn_w_up, ffn_w_down, moe_router, moe_w_gate,
                         moe_w_up, moe_w_down)
    bp, bs = c_prompt.shape[0], c_sample.shape[0]
    rows = -(-(bp + bs) // 8) * 8
    c = jnp.zeros((rows, D_MODEL), F32).at[:bp].set(c_prompt).at[bp:bp + bs].set(c_sample)
    mod = _small_matmul(c, ada_w, ada_b.reshape(DEPTH, 1, 6 * D_MODEL), act=True, col_tile=D_MODEL)
    mod = mod[:, :bp + bs].reshape(DEPTH, bp + bs, 6, D_MODEL)
    return _trunks((x_prompt, x_sample), mod, norm_gains, prep)
```

```python
import functools

import jax
import jax.numpy as jnp
from jax import lax
from jax.experimental import pallas as pl
from jax.experimental.pallas import tpu as pltpu

D_MODEL = 1024
DEPTH = 4
GRID_W = 64
GLA_HEADS = 4
GLA_DK = D_MODEL // 2
GLA_DV = D_MODEL
GLA_DK_HEAD = GLA_DK // GLA_HEADS
GLA_DV_HEAD = GLA_DV // GLA_HEADS
GLA_GATE_RANK = 16
GLA_TAU = 16.0
GLA_CHUNK = 64
NA_HEADS = 16
NA_HEAD_DIM = D_MODEL // NA_HEADS
WIN_ROWS = 8
WIN_COLS = 16
D_FF = 7 * D_MODEL // 2
N_EXPERTS = 8
TOP_K = 2
NORM_EPS = 1e-6
NEG_INF = -1e30
LOG2E = 1.4426950408889634

BF16 = jnp.bfloat16
F32 = jnp.float32

VMEM_LIMIT_BYTES = 56 * 1024 * 1024
LANES = 128
SUBLANES = 8

ROW_TILE = 512
FFN_ROW_TILE = 1024
FFN_COL_TILE = 512
GLA_TIME_BLOCK = 2048
NA_Q_ROWS = 4
NA_K_ROWS = 12
NA_TILES_PER_STEP = 4
MOE_BLOCK = 1024
MOE_ISSUE_STEPS = D_FF // FFN_COL_TILE

def _cparams(*sem):
    return pltpu.CompilerParams(dimension_semantics=sem, vmem_limit_bytes=VMEM_LIMIT_BYTES)


def _norm_mod(x, gain, scale, shift):
    ms = jnp.mean(x * x, axis=-1, keepdims=True)
    y = x * lax.rsqrt(ms + NORM_EPS) * gain
    return y * (1.0 + scale) + shift


def _rms_gain(y, gain):
    ms = jnp.mean(y * y, axis=-1, keepdims=True)
    return y * lax.rsqrt(ms + NORM_EPS) * gain


def _dot(a, b):
    return jnp.dot(a, b, preferred_element_type=F32)


def _dot_nt(a, b):
    return lax.dot_general(a, b, (((1,), (1,)), ((), ())), preferred_element_type=F32)


def _dot_tn(a, b):
    return lax.dot_general(a, b, (((0,), (0,)), ((), ())), preferred_element_type=F32)


def _silu(x):
    return x * (1.0 / (1.0 + jnp.exp(-x)))


def _small_matmul_kernel(a_ref, w_ref, b_ref, o_ref, *, act):
    a = a_ref[...]
    if act:
        a = _silu(a)
    o_ref[0] = jnp.dot(a, w_ref[0], preferred_element_type=F32,
                       precision=lax.Precision.HIGHEST) + b_ref[0]


def _small_matmul(a, w, b, *, act, col_tile):
    M, K = a.shape
    G, _, N = w.shape
    return pl.pallas_call(
        functools.partial(_small_matmul_kernel, act=act),
        grid=(G, N // col_tile),
        in_specs=[pl.BlockSpec((M, K), lambda g, j: (0, 0)),
                  pl.BlockSpec((1, K, col_tile), lambda g, j: (g, 0, j)),
                  pl.BlockSpec((1, 1, col_tile), lambda g, j: (g, 0, j))],
        out_specs=pl.BlockSpec((1, M, col_tile), lambda g, j: (g, 0, j)),
        out_shape=jax.ShapeDtypeStruct((G, M, N), F32),
        compiler_params=_cparams("parallel", "parallel"),
        name="small_matmul",
    )(a, w, b)


def _proj_kernel(x_ref, g_ref, mod_ref, w_ref, zb_ref, o_ref, z_ref, *, n_main, n_z, cn):
    m = mod_ref[0]
    h = _norm_mod(x_ref[...], g_ref[...], m[1:2], m[0:1]).astype(BF16)
    for n in range(0, n_main, cn):
        o_ref[:, n:n + cn] = _dot(h, w_ref[:, n:n + cn]).astype(o_ref.dtype)
    for n in range(0, n_z, cn):
        z_ref[:, n:n + cn] = _dot(h, w_ref[:, n_main + n:n_main + n + cn]) + zb_ref[:, n:n + cn]


def _proj_kernel_noz(x_ref, g_ref, mod_ref, w_ref, o_ref, *, n_main, cn):
    m = mod_ref[0]
    h = _norm_mod(x_ref[...], g_ref[...], m[1:2], m[0:1]).astype(BF16)
    for n in range(0, n_main, cn):
        o_ref[:, n:n + cn] = _dot(h, w_ref[:, n:n + cn]).astype(o_ref.dtype)


def _proj(x, gain, mod, w, zb, L, n_main, n_z):
    T = x.shape[0]
    tm = min(ROW_TILE, L)
    spt = L // tm
    N = n_main + n_z
    in_specs = [pl.BlockSpec((tm, D_MODEL), lambda i: (i, 0)),
                pl.BlockSpec((1, D_MODEL), lambda i: (0, 0)),
                pl.BlockSpec((1, 6, D_MODEL), lambda i: (i // spt, 0, 0)),
                pl.BlockSpec((D_MODEL, N), lambda i: (0, 0))]
    if n_z:
        return pl.pallas_call(
            functools.partial(_proj_kernel, n_main=n_main, n_z=n_z, cn=512),
            grid=(T // tm,),
            in_specs=in_specs + [pl.BlockSpec((1, n_z), lambda i: (0, 0))],
            out_specs=[pl.BlockSpec((tm, n_main), lambda i: (i, 0)),
                       pl.BlockSpec((tm, n_z), lambda i: (i, 0))],
            out_shape=[jax.ShapeDtypeStruct((T, n_main), BF16),
                       jax.ShapeDtypeStruct((T, n_z), F32)],
            compiler_params=_cparams("parallel"),
            name="norm_proj_gla",
        )(x, gain, mod, w, zb)
    return pl.pallas_call(
        functools.partial(_proj_kernel_noz, n_main=n_main, cn=512),
        grid=(T // tm,),
        in_specs=in_specs,
        out_specs=pl.BlockSpec((tm, n_main), lambda i: (i, 0)),
        out_shape=jax.ShapeDtypeStruct((T, n_main), BF16),
        compiler_params=_cparams("parallel"),
        name="norm_proj_na",
    )(x, gain, mod, w)


def _outproj_kernel(o_ref, w_ref, x_ref, g_ref, mod_ref, y_ref):
    y = _dot(o_ref[...], w_ref[...])
    gate = mod_ref[0][2:3]
    y_ref[...] = x_ref[...] + gate * _rms_gain(y, g_ref[...])


def _outproj(o, w, x, gain, mod, L):
    T, K = o.shape
    tm = min(ROW_TILE, L)
    spt = L // tm
    return pl.pallas_call(
        _outproj_kernel,
        grid=(T // tm,),
        in_specs=[pl.BlockSpec((tm, K), lambda i: (i, 0)),
                  pl.BlockSpec((K, D_MODEL), lambda i: (0, 0)),
                  pl.BlockSpec((tm, D_MODEL), lambda i: (i, 0)),
                  pl.BlockSpec((1, D_MODEL), lambda i: (0, 0)),
                  pl.BlockSpec((1, 6, D_MODEL), lambda i: (i // spt, 0, 0))],
        out_specs=pl.BlockSpec((tm, D_MODEL), lambda i: (i, 0)),
        out_shape=jax.ShapeDtypeStruct((T, D_MODEL), F32),
        compiler_params=_cparams("parallel"),
        name="outproj_residual",
    )(o, w, x, gain, mod)


def _ffn_kernel(o_ref, wo_ref, x_ref, g1_ref, g2_ref, g3_ref, mod_ref, wg_ref, wu_ref, wd_ref,
                y_ref, h_ref, acc_ref):
    f = pl.program_id(1)
    m = mod_ref[0]

    @pl.when(f == 0)
    def _():
        x1 = x_ref[...] + m[2:3] * _rms_gain(_dot(o_ref[...], wo_ref[...]), g1_ref[...])
        y_ref[...] = x1
        h_ref[...] = _norm_mod(x1, g2_ref[...], m[4:5], m[3:4]).astype(BF16)
        acc_ref[...] = jnp.zeros_like(acc_ref)

    h = h_ref[...]
    a = (_silu(_dot(h, wg_ref[...])) * _dot(h, wu_ref[...])).astype(BF16)
    acc_ref[...] += _dot(a, wd_ref[...])

    @pl.when(f == pl.num_programs(1) - 1)
    def _():
        y_ref[...] = y_ref[...] + m[5:6] * _rms_gain(acc_ref[...], g3_ref[...])


def _ffn(o, wo, x, gain1, gain2, gain3, mod, wg, wu, wd, L):
    T = x.shape[0]
    tm = min(FFN_ROW_TILE, L)
    spt = L // tm
    tf = FFN_COL_TILE
    return pl.pallas_call(
        _ffn_kernel,
        grid=(T // tm, D_FF // tf),
        in_specs=[pl.BlockSpec((tm, D_MODEL), lambda i, f: (i, 0)),
                  pl.BlockSpec((D_MODEL, D_MODEL), lambda i, f: (0, 0)),
                  pl.BlockSpec((tm, D_MODEL), lambda i, f: (i, 0)),
                  pl.BlockSpec((1, D_MODEL), lambda i, f: (0, 0)),
                  pl.BlockSpec((1, D_MODEL), lambda i, f: (0, 0)),
                  pl.BlockSpec((1, D_MODEL), lambda i, f: (0, 0)),
                  pl.BlockSpec((1, 6, D_MODEL), lambda i, f: (i // spt, 0, 0)),
                  pl.BlockSpec((D_MODEL, tf), lambda i, f: (0, f)),
                  pl.BlockSpec((D_MODEL, tf), lambda i, f: (0, f)),
                  pl.BlockSpec((tf, D_MODEL), lambda i, f: (f, 0))],
        out_specs=pl.BlockSpec((tm, D_MODEL), lambda i, f: (i, 0)),
        out_shape=jax.ShapeDtypeStruct((T, D_MODEL), F32),
        scratch_shapes=[pltpu.VMEM((tm, D_MODEL), BF16), pltpu.VMEM((tm, D_MODEL), F32)],
        compiler_params=_cparams("parallel", "arbitrary"),
        name="ffn_swiglu",
    )(o, wo, x, gain1, gain2, gain3, mod, wg, wu, wd)


def _gla_block(q_ref, k_ref, v_ref, z_ref, S, *, tb, reverse):
    C = GLA_CHUNK
    chunks = list(range(tb // C))
    order = chunks[::-1] if reverse else chunks
    sl = {c: slice(c * C, (c + 1) * C) for c in chunks}
    row = lax.broadcasted_iota(jnp.int32, (C, C), 0)
    col = lax.broadcasted_iota(jnp.int32, (C, C), 1)
    if reverse:
        tri = col >= row
        allowed = col > row
        i_ref, i_last = C - 1 - C // 2, 0
    else:
        tri = row >= col
        allowed = row >= col
        i_ref, i_last = C // 2, C - 1
    tri = jnp.where(tri, 1.0, 0.0).astype(BF16)
    dk = q_ref.shape[1]
    reps = v_ref.shape[1] // dk

    z = z_ref[...]
    g = (jnp.minimum(z, 0.0) - jnp.log(1.0 + jnp.exp(-jnp.abs(z)))) * (LOG2E / GLA_TAU)
    g_hi = g.astype(BF16)
    g_lo = (g - g_hi.astype(F32)).astype(BF16)
    g_terms = jnp.concatenate([g_hi, g_lo], axis=1)
    G = {}
    for c in chunks:
        t = _dot(tri, g_terms[sl[c]])
        G[c] = t[:, :dk] + t[:, dk:]
    q_rel, k_rel, q_in, k_out, decay = {}, {}, {}, {}, {}
    for c in chunks:
        q = q_ref[sl[c], :].astype(F32)
        k = k_ref[sl[c], :].astype(F32)
        G_ref = G[c][i_ref:i_ref + 1]
        G_last = G[c][i_last:i_last + 1]
        q_rel[c] = (q * jnp.exp2(G[c] - G_ref)).astype(BF16)
        k_rel[c] = (k * jnp.exp2(G_ref - G[c])).astype(BF16)
        q_in[c] = (q * jnp.exp2(G[c])).astype(BF16)
        k_out[c] = (k * jnp.exp2(G_last - G[c])).astype(BF16)
        d = jnp.broadcast_to(jnp.exp2(G_last), (dk, dk)).T
        decay[c] = jnp.concatenate([d] * reps, axis=1)
    scores = {c: jnp.where(allowed, _dot_nt(q_rel[c], k_rel[c]), 0.0).astype(BF16) for c in chunks}
    kv = {c: _dot_tn(k_out[c], v_ref[sl[c], :]) for c in chunks}
    S_in = {}
    for c in order:
        S_in[c] = S.astype(BF16)
        S = decay[c] * S + kv[c]
    o = {c: _dot(scores[c], v_ref[sl[c], :]) + _dot(q_in[c], S_in[c]) for c in chunks}
    return o, S


def _gla_fwd_kernel(q_ref, k_ref, v_ref, z_ref, o_ref, s_ref, *, tb):
    @pl.when(pl.program_id(2) == 0)
    def _():
        s_ref[...] = jnp.zeros_like(s_ref)

    o, S = _gla_block(q_ref, k_ref, v_ref, z_ref, s_ref[...], tb=tb, reverse=False)
    for c, oc in o.items():
        o_ref[c * GLA_CHUNK:(c + 1) * GLA_CHUNK, :] = oc
    s_ref[...] = S


def _gla_bwd_kernel(q_ref, k_ref, v_ref, z_ref, of_ref, r_ref, gain_ref, o_ref, s_ref, *, tb):
    @pl.when(pl.program_id(2) == 0)
    def _():
        s_ref[...] = jnp.zeros_like(s_ref)

    o, S = _gla_block(q_ref, k_ref, v_ref, z_ref, s_ref[...], tb=tb, reverse=True)
    for c, oc in o.items():
        sl = slice(c * GLA_CHUNK, (c + 1) * GLA_CHUNK)
        oc = _rms_gain(oc + of_ref[sl, :], gain_ref[...])
        o_ref[sl, :] = (oc * _silu(r_ref[sl, :].astype(F32))).astype(o_ref.dtype)
    s_ref[...] = S


def _gla_core(main, z, gla_norm, B, L):
    T = B * L
    tb = min(GLA_TIME_BLOCK, L)
    nt = L // tb
    H = GLA_HEADS
    dk, dv = GLA_DK_HEAD, GLA_DV_HEAD
    grid = (B, H, nt)
    scratch = [pltpu.VMEM((dk, dv), F32)]

    def fwd_t(b, h, t):
        return b * nt + t

    def bwd_t(b, h, t):
        return b * nt + (nt - 1 - t)

    o_fwd = pl.pallas_call(
        functools.partial(_gla_fwd_kernel, tb=tb),
        grid=grid,
        in_specs=[pl.BlockSpec((tb, dk), lambda b, h, t: (fwd_t(b, h, t), h)),
                  pl.BlockSpec((tb, dk), lambda b, h, t: (fwd_t(b, h, t), H + h)),
                  pl.BlockSpec((tb, dv), lambda b, h, t: (fwd_t(b, h, t), H + h)),
                  pl.BlockSpec((tb, dk), lambda b, h, t: (fwd_t(b, h, t), h))],
        out_specs=pl.BlockSpec((tb, dv), lambda b, h, t: (fwd_t(b, h, t), h)),
        out_shape=jax.ShapeDtypeStruct((T, GLA_DV), F32),
        scratch_shapes=scratch,
        compiler_params=_cparams("parallel", "parallel", "arbitrary"),
        name="gla_forward",
    )(main, main, main, z)
    return pl.pallas_call(
        functools.partial(_gla_bwd_kernel, tb=tb),
        grid=grid,
        in_specs=[pl.BlockSpec((tb, dk), lambda b, h, t: (bwd_t(b, h, t), h)),
                  pl.BlockSpec((tb, dk), lambda b, h, t: (bwd_t(b, h, t), H + h)),
                  pl.BlockSpec((tb, dv), lambda b, h, t: (bwd_t(b, h, t), H + h)),
                  pl.BlockSpec((tb, dk), lambda b, h, t: (bwd_t(b, h, t), H + h)),
                  pl.BlockSpec((tb, dv), lambda b, h, t: (bwd_t(b, h, t), h)),
                  pl.BlockSpec((tb, dv), lambda b, h, t: (bwd_t(b, h, t), 2 * H + h)),
                  pl.BlockSpec((1, dv), lambda b, h, t: (0, h))],
        out_specs=pl.BlockSpec((tb, dv), lambda b, h, t: (bwd_t(b, h, t), h)),
        out_shape=jax.ShapeDtypeStruct((T, GLA_DV), BF16),
        scratch_shapes=scratch,
        compiler_params=_cparams("parallel", "parallel", "arbitrary"),
        name="gla_backward_finish",
    )(main, main, main, z, o_fwd, main, gla_norm)


def _na_bias_table(rpb):
    cols = jnp.arange(GRID_W)
    col_start = jnp.clip(cols - WIN_COLS // 2, 0, GRID_W - WIN_COLS)
    in_win = (cols[None, :] >= col_start[:, None]) & (cols[None, :] < col_start[:, None] + WIN_COLS)
    dc_idx = jnp.clip(cols[None, :] - cols[:, None] + WIN_COLS - 1, 0, 2 * WIN_COLS - 2)
    pick_c = (dc_idx[:, :, None] == jnp.arange(2 * WIN_COLS - 1)).astype(F32)
    a = jnp.arange(NA_Q_ROWS)
    j = jnp.arange(NA_K_ROWS)
    tables = []
    for delta, win_start in ((0, jnp.zeros_like(a)), (4, a), (8, jnp.full_like(a, 4))):
        live_row = (j[None, :] >= win_start[:, None]) & (j[None, :] < win_start[:, None] + WIN_ROWS)
        dr_idx = jnp.clip(j[None, :] - delta - a[:, None] + WIN_ROWS - 1, 0, 2 * WIN_ROWS - 2)
        pick_r = (dr_idx[:, :, None] == jnp.arange(2 * WIN_ROWS - 1)).astype(F32)
        bias = jnp.einsum('hrc,ajr,qkc->haqjk', rpb.astype(F32), pick_r, pick_c,
                          precision=lax.Precision.HIGHEST) * LOG2E
        live = live_row[:, None, :, None] & in_win[None, :, None, :]
        tables.append(jnp.where(live[None], bias, NEG_INF))
    t = jnp.stack(tables)
    return t.reshape(3, NA_HEADS, NA_Q_ROWS * GRID_W, NA_K_ROWS * GRID_W)


def _na_kernel(q_ref, k_ref, v_ref, bias_ref, o_ref, *, rows):
    ntile = rows // NA_Q_ROWS
    tq = NA_Q_ROWS * GRID_W
    tk = NA_K_ROWS * GRID_W
    lane = lax.broadcasted_iota(jnp.int32, (1, LANES), 1)
    first_head = lane < NA_HEAD_DIM

    def body(tt, carry):
        q0, geo, k, v, s = {}, {}, {}, {}, {}
        units = [(u, hh) for u in range(NA_TILES_PER_STEP) for hh in range(2)]
        for u in range(NA_TILES_PER_STEP):
            t = NA_TILES_PER_STEP * tt + u
            kr0 = jnp.clip(NA_Q_ROWS * t - 4, 0, rows - NA_K_ROWS)
            geo[u] = jnp.where(t == 0, 0, jnp.where(t == ntile - 1, 2, 1))
            q0[u] = pl.multiple_of(t * tq, tq)
            k0 = pl.multiple_of(kr0 * GRID_W, GRID_W)
            q = q_ref[pl.ds(q0[u], tq), :]
            k[u] = k_ref[pl.ds(k0, tk), :]
            v[u] = v_ref[pl.ds(k0, tk), :]
            for hh in range(2):
                sel = first_head if hh == 0 else jnp.logical_not(first_head)
                s[u, hh] = _dot_nt(jnp.where(sel, q, jnp.zeros_like(q)), k[u])
        p, inv_l, o = {}, {}, {}
        for u, hh in units:
            sb = s[u, hh] + bias_ref[geo[u], hh]
            e = jnp.exp2(sb - jnp.max(sb, axis=-1, keepdims=True))
            inv_l[u, hh] = 1.0 / jnp.sum(e, axis=-1, keepdims=True)
            p[u, hh] = e.astype(BF16)
        for u, hh in units:
            o[u, hh] = _dot(p[u, hh], v[u]) * inv_l[u, hh]
        for u in range(NA_TILES_PER_STEP):
            o_ref[pl.ds(q0[u], tq), :] = jnp.where(first_head, o[u, 0], o[u, 1]).astype(o_ref.dtype)
        return carry

    lax.fori_loop(0, ntile // NA_TILES_PER_STEP, body, 0)


def _na_core(qkv, bias, B, L):
    T = B * L
    rows = L // GRID_W
    npair = NA_HEADS // 2
    tq = NA_Q_ROWS * GRID_W
    tk = NA_K_ROWS * GRID_W
    assert (rows // NA_Q_ROWS) % NA_TILES_PER_STEP == 0 and rows >= NA_K_ROWS
    return pl.pallas_call(
        functools.partial(_na_kernel, rows=rows),
        grid=(npair, B),
        in_specs=[pl.BlockSpec((L, LANES), lambda p, b: (b, p)),
                  pl.BlockSpec((L, LANES), lambda p, b: (b, npair + p)),
                  pl.BlockSpec((L, LANES), lambda p, b: (b, 2 * npair + p)),
                  pl.BlockSpec((3, 2, tq, tk), lambda p, b: (0, p, 0, 0))],
        out_specs=pl.BlockSpec((L, LANES), lambda p, b: (b, p)),
        out_shape=jax.ShapeDtypeStruct((T, D_MODEL), BF16),
        compiler_params=_cparams("parallel", "parallel"),
        name="neighbourhood_attention",
    )(qkv, qkv, qkv, bias)


def _router_kernel(xa_ref, xb_ref, g_ref, mod_ref, wr_ref, h_ref, route_ref, cnt_ref, carry_ref, *,
                   na):
    i = pl.program_id(0)

    @pl.when(i == 0)
    def _():
        carry_ref[...] = jnp.zeros_like(carry_ref)

    m = mod_ref[0]
    x = jnp.where(i < na, xa_ref[...], xb_ref[...])
    h = _norm_mod(x, g_ref[...], m[4:5], m[3:4])
    tm = h.shape[0]
    for s in range(SUBLANES):
        h_ref[pl.ds(s, tm, stride=SUBLANES), :] = h[:, s * LANES:(s + 1) * LANES]
    h_hi = h.astype(BF16)
    h_lo = (h - h_hi.astype(F32)).astype(BF16)
    both = _dot(h_hi, wr_ref[...])
    logits = both[:, :LANES] + both[:, LANES:] + _dot(h_lo, wr_ref[:, :LANES])
    lane = lax.broadcasted_iota(jnp.int32, logits.shape, 1)
    valid = lane < N_EXPERTS
    l1 = jnp.where(valid, logits, -jnp.inf)
    m1 = jnp.max(l1, axis=-1, keepdims=True)
    i1 = jnp.min(jnp.where(l1 == m1, lane, LANES), axis=-1, keepdims=True)
    l2 = jnp.where(lane == i1, -jnp.inf, l1)
    m2 = jnp.max(l2, axis=-1, keepdims=True)
    i2 = jnp.min(jnp.where(l2 == m2, lane, LANES), axis=-1, keepdims=True)
    e = jnp.exp(m2 - m1)
    g1 = 1.0 / (1.0 + e)
    g2 = e * g1
    chosen = (lane == i1) | (lane == i2)
    carry_ref[...] = carry_ref[...] + jnp.sum(jnp.where(chosen, 1.0, 0.0), axis=0, keepdims=True)
    cnt_ref[...] = carry_ref[...]
    route = jnp.where(lane == 0, i1.astype(F32),
                      jnp.where(lane == 1, i2.astype(F32),
                                jnp.where(lane == 2, g1, jnp.where(lane == 3, g2, 0.0))))
    route_ref[...] = route


class _TwoTrunks:
    def __init__(self, xa, xb, La, Lb, tm):
        self.na = xa.shape[0] // tm
        self.nb = xb.shape[0] // tm
        self.T = xa.shape[0] + xb.shape[0]
        self.tm = tm
        seqs_a = xa.shape[0] // La
        tiles_per_seq_a = La // tm
        tiles_per_seq_b = Lb // tm
        na = self.na
        self.a_tile = lambda i: (jnp.minimum(i, na - 1), 0)
        self.b_tile = lambda i: (jnp.maximum(i - na, 0), 0)
        self.seq = lambda i: (jnp.where(i < na, i // tiles_per_seq_a,
                                        seqs_a + (i - na) // tiles_per_seq_b), 0, 0)


def _router(xa, xb, gain, mod, wr, La, Lb):
    tm = min(ROW_TILE, La, Lb)
    tt = _TwoTrunks(xa, xb, La, Lb, tm)
    T = tt.T
    return pl.pallas_call(
        functools.partial(_router_kernel, na=tt.na),
        grid=(T // tm,),
        in_specs=[pl.BlockSpec((tm, D_MODEL), tt.a_tile),
                  pl.BlockSpec((tm, D_MODEL), tt.b_tile),
                  pl.BlockSpec((1, D_MODEL), lambda i: (0, 0)),
                  pl.BlockSpec((1, 6, D_MODEL), tt.seq),
                  pl.BlockSpec((D_MODEL, 2 * LANES), lambda i: (0, 0))],
        out_specs=[pl.BlockSpec((tm * SUBLANES, LANES), lambda i: (i, 0)),
                   pl.BlockSpec((tm, LANES), lambda i: (i, 0)),
                   pl.BlockSpec((SUBLANES, LANES), lambda i: (0, 0))],
        out_shape=[jax.ShapeDtypeStruct((T * SUBLANES, LANES), F32),
                   jax.ShapeDtypeStruct((T, LANES), F32),
                   jax.ShapeDtypeStruct((SUBLANES, LANES), F32)],
        scratch_shapes=[pltpu.VMEM((SUBLANES, LANES), F32)],
        compiler_params=_cparams("arbitrary"),
        name="moe_router",
    )(xa, xb, gain, mod, wr)


def _moe_kernel(be_ref, nused_ref, srcx_hbm, dstx_hbm, h_hbm, wg_ref, wu_ref, wd_ref, out_hbm,
                src_smem, dst_smem, xbuf, xb16, acc_ref, ybuf,
                src_sem, dst_sem, gat_sem, sca_sem, *, bm):
    b = pl.program_id(0)
    f = pl.program_id(1)
    nb = pl.num_programs(0)
    nf = pl.num_programs(1)
    slot = b % 2
    nxt = 1 - slot
    group = -(-bm // MOE_ISSUE_STEPS)
    always = bm - (MOE_ISSUE_STEPS - 1) * group

    def src_copy(r, s):
        return pltpu.make_async_copy(srcx_hbm.at[r], src_smem.at[s], src_sem.at[s])

    def dst_copy(r, s):
        return pltpu.make_async_copy(dstx_hbm.at[r], dst_smem.at[s], dst_sem.at[s])

    def tile(ref, i):
        return ref.at[pl.ds(pl.multiple_of(i * SUBLANES, SUBLANES), SUBLANES)]

    def gather_row(i, s):
        return pltpu.make_async_copy(tile(h_hbm, src_smem[s, i]), tile(xbuf.at[s], i), gat_sem.at[s])

    def scatter_row(i, s):
        return pltpu.make_async_copy(tile(ybuf, i), tile(out_hbm, dst_smem[s, i]), sca_sem)

    def wait_gather(s):
        pltpu.make_async_copy(h_hbm.at[pl.ds(0, bm * SUBLANES)], xbuf.at[s], gat_sem.at[s]).wait()

    def wait_scatter():
        pltpu.make_async_copy(ybuf, out_hbm.at[pl.ds(0, bm * SUBLANES)], sca_sem).wait()

    def for_all_rows(start_row):
        def issue(i, c):
            start_row(i)
            return c
        lax.fori_loop(0, bm, issue, 0)

    @pl.when(f == 0)
    def _():
        @pl.when(b == 0)
        def _():
            src_copy(0, 0).start()
            src_copy(1, 1).start()
            dst_copy(0, 0).start()
            src_copy(0, 0).wait()
            for_all_rows(lambda i: gather_row(i, 0).start())
            ybuf[...] = jnp.zeros_like(ybuf)

        src_copy(b + 1, nxt).wait()
        dst_copy(b, slot).wait()
        wait_gather(slot)
        for s in range(SUBLANES):
            xb16[:, s * LANES:(s + 1) * LANES] = xbuf[slot, pl.ds(s, bm, stride=SUBLANES), :].astype(BF16)
        acc_ref[...] = jnp.zeros_like(acc_ref)

    base = f * group
    used = b < nused_ref[0]

    @pl.when(used)
    def _():
        for j in range(always):
            gather_row(base + j, nxt).start()
            scatter_row(base + j, slot).start()
        x = xb16[...]
        a = (_silu(_dot(x, wg_ref[0])) * _dot(x, wu_ref[0])).astype(BF16)
        acc_ref[...] += _dot(a, wd_ref[0])

    @pl.when(jnp.logical_not(used))
    def _():
        def issue(j, c):
            gather_row(base + j, nxt).start()
            scatter_row(base + j, slot).start()
            return c
        lax.fori_loop(0, always, issue, 0)

    @pl.when(f < nf - 1)
    def _():
        for j in range(always, group):
            gather_row(base + j, nxt).start()
            scatter_row(base + j, slot).start()

    @pl.when(f == nf - 1)
    def _():
        wait_scatter()
        for s in range(SUBLANES):
            ybuf[pl.ds(s, bm, stride=SUBLANES), :] = acc_ref[:, s * LANES:(s + 1) * LANES]
        dst_copy(b + 1, nxt).start()

        @pl.when(b + 2 <= nb)
        def _():
            src_copy(b + 2, slot).start()

        @pl.when(b == nb - 1)
        def _():
            dst_copy(b + 1, nxt).wait()
            for_all_rows(lambda i: scatter_row(i, nxt).start())
            wait_scatter()
            wait_gather(nxt)


def _moe_experts(block_e, nused, srcx, dstx, h, wg, wu, wd, n_rows_out):
    n_blocks = srcx.shape[0] - 1
    bm = srcx.shape[1]
    tf = FFN_COL_TILE
    assert D_FF // tf == MOE_ISSUE_STEPS
    grid_spec = pltpu.PrefetchScalarGridSpec(
        num_scalar_prefetch=2,
        grid=(n_blocks, D_FF // tf),
        in_specs=[pl.BlockSpec(memory_space=pl.ANY),
                  pl.BlockSpec(memory_space=pl.ANY),
                  pl.BlockSpec(memory_space=pl.ANY),
                  pl.BlockSpec((1, D_MODEL, tf), lambda b, f, be, nu: (be[b], 0, f)),
                  pl.BlockSpec((1, D_MODEL, tf), lambda b, f, be, nu: (be[b], 0, f)),
                  pl.BlockSpec((1, tf, D_MODEL), lambda b, f, be, nu: (be[b], f, 0))],
        out_specs=pl.BlockSpec(memory_space=pl.ANY),
        scratch_shapes=[pltpu.SMEM((2, bm), jnp.int32),
                        pltpu.SMEM((2, bm), jnp.int32),
                        pltpu.VMEM((2, bm * SUBLANES, LANES), F32),
                        pltpu.VMEM((bm, D_MODEL), BF16),
                        pltpu.VMEM((bm, D_MODEL), F32),
                        pltpu.VMEM((bm * SUBLANES, LANES), F32),
                        pltpu.SemaphoreType.DMA((2,)),
                        pltpu.SemaphoreType.DMA((2,)),
                        pltpu.SemaphoreType.DMA((2,)),
                        pltpu.SemaphoreType.DMA(())],
    )
    return pl.pallas_call(
        functools.partial(_moe_kernel, bm=bm),
        grid_spec=grid_spec,
        out_shape=jax.ShapeDtypeStruct((n_rows_out * SUBLANES, LANES), F32),
        compiler_params=_cparams("arbitrary", "arbitrary"),
        name="moe_experts",
    )(block_e, nused, srcx, dstx, h, wg, wu, wd)


def _combine_kernel(y1_ref, y2_ref, route_ref, xa_ref, xb_ref, g_ref, mod_ref, oa_ref, ob_ref, y_ref,
                    *, na):
    i = pl.program_id(0)
    r = route_ref[...]
    tm = r.shape[0]
    for s in range(SUBLANES):
        rows = pl.ds(s, tm, stride=SUBLANES)
        y_ref[:, s * LANES:(s + 1) * LANES] = r[:, 2:3] * y1_ref[rows, :] + r[:, 3:4] * y2_ref[rows, :]
    upd = mod_ref[0][5:6] * _rms_gain(y_ref[...], g_ref[...])

    @pl.when(i < na)
    def _():
        oa_ref[...] = xa_ref[...] + upd

    @pl.when(i >= na)
    def _():
        ob_ref[...] = xb_ref[...] + upd


def _moe_combine(y, route, xa, xb, gain, mod, La, Lb):
    tm = min(ROW_TILE, La, Lb)
    tt = _TwoTrunks(xa, xb, La, Lb, tm)
    nt = tt.T // tm
    return pl.pallas_call(
        functools.partial(_combine_kernel, na=tt.na),
        grid=(nt,),
        in_specs=[pl.BlockSpec((tm * SUBLANES, LANES), lambda i: (i, 0)),
                  pl.BlockSpec((tm * SUBLANES, LANES), lambda i: (nt + i, 0)),
                  pl.BlockSpec((tm, LANES), lambda i: (i, 0)),
                  pl.BlockSpec((tm, D_MODEL), tt.a_tile),
                  pl.BlockSpec((tm, D_MODEL), tt.b_tile),
                  pl.BlockSpec((1, D_MODEL), lambda i: (0, 0)),
                  pl.BlockSpec((1, 6, D_MODEL), tt.seq)],
        out_specs=[pl.BlockSpec((tm, D_MODEL), tt.a_tile),
                   pl.BlockSpec((tm, D_MODEL), tt.b_tile)],
        out_shape=[jax.ShapeDtypeStruct(xa.shape, F32), jax.ShapeDtypeStruct(xb.shape, F32)],
        scratch_shapes=[pltpu.VMEM((tm, D_MODEL), F32)],
        compiler_params=_cparams("arbitrary"),
        name="moe_combine_residual",
    )(y, y, route, xa, xb, gain, mod)


def _moe_plan(route, counts, T, bm):
    n_assign = TOP_K * T
    n_blocks = n_assign // bm + N_EXPERTS
    experts = jnp.arange(N_EXPERTS, dtype=jnp.int32)
    e_flat = jnp.concatenate([route[:, k] for k in range(TOP_K)]).astype(jnp.int32)
    _, order = lax.sort_key_val(e_flat, jnp.arange(n_assign, dtype=jnp.int32))
    counts = counts[0, :N_EXPERTS].astype(jnp.int32)
    padded = ((counts + bm - 1) // bm) * bm
    start = jnp.cumsum(counts) - counts
    pend = jnp.cumsum(padded)
    pstart = pend - padded
    pos = jnp.arange(n_blocks * bm, dtype=jnp.int32)
    pe = jnp.minimum(jnp.sum(pos[:, None] >= pend[None, :], axis=1), N_EXPERTS - 1).astype(jnp.int32)
    mine = pe[:, None] == experts[None, :]

    def of_expert(table):
        return jnp.sum(jnp.where(mine, table[None, :], 0), axis=1).astype(jnp.int32)

    rank = pos - of_expert(pstart)
    valid = rank < of_expert(counts)
    a = jnp.take(order, jnp.clip(of_expert(start) + rank, 0, n_assign - 1))
    spare = (n_assign + pos % bm).astype(jnp.int32)
    dst = jnp.where(valid, a, spare)
    src = jnp.where(valid, a % T, 0)
    block_e = pe[::bm]
    srcx = jnp.concatenate([src.reshape(n_blocks, bm), jnp.zeros((1, bm), jnp.int32)], axis=0)
    dstx = jnp.concatenate([spare[:bm].reshape(1, bm), dst.reshape(n_blocks, bm)], axis=0)
    nused = (pend[N_EXPERTS - 1] // bm).astype(jnp.int32).reshape(1)
    return block_e, nused, srcx, dstx


def _moe_layer(xa, xb, gain2, gain3, mod, wr, wg, wu, wd, La, Lb):
    T = xa.shape[0] + xb.shape[0]
    bm = min(MOE_BLOCK, T)
    h, route, counts = _router(xa, xb, gain2, mod, wr, La, Lb)
    block_e, nused, srcx, dstx = _moe_plan(route, counts, T, bm)
    y = _moe_experts(block_e, nused, srcx, dstx, h, wg, wu, wd, TOP_K * T + bm)
    return _moe_combine(y, route, xa, xb, gain3, mod, La, Lb)


def _prep_weights(gla_w_in, gla_w_gate_up, gla_b_gate, gla_norm, gla_w_out, na_w_qkv, na_rpb,
                  na_w_out, ffn_w_gate, ffn_w_up, ffn_w_down, moe_router, moe_w_gate, moe_w_up,
                  moe_w_down):
    n_a = gla_w_in.shape[0]
    n_b = na_w_qkv.shape[0]
    R = GLA_GATE_RANK
    n_qkvr = 2 * GLA_DK + 2 * GLA_DV
    prep = {'gla': [], 'na': [], 'ffn': [], 'moe': []}
    for j in range(n_a):
        w = gla_w_in[j]
        w1 = jnp.concatenate([w[:, n_qkvr:n_qkvr + R], w[:, n_qkvr + R:]], axis=0)
        w1 = jnp.pad(w1, ((0, 0), (0, LANES - R)))
        w2 = jnp.pad(gla_w_gate_up[j], ((0, 0), (0, LANES - R), (0, 0)))
        fold = _small_matmul(w1, w2, jnp.zeros((2, 1, GLA_DK), F32),
                             act=False, col_tile=GLA_DK)
        w_z = jnp.concatenate([fold[0, :D_MODEL], fold[1, D_MODEL:]], axis=1)
        w_q = w[:, :GLA_DK] * (GLA_DK_HEAD ** -0.5)
        w_all = jnp.concatenate([w_q, w[:, GLA_DK:n_qkvr], w_z], axis=1).astype(BF16)
        prep['gla'].append(dict(w_all=w_all, zb=gla_b_gate[j].reshape(1, 2 * GLA_DK),
                                norm=gla_norm[j].reshape(1, GLA_DV),
                                w_out=gla_w_out[j].astype(BF16)))
        prep['ffn'].append(dict(wg=ffn_w_gate[j].astype(BF16), wu=ffn_w_up[j].astype(BF16),
                                wd=ffn_w_down[j].astype(BF16)))
    for j in range(n_b):
        w = na_w_qkv[j]
        w_qkv = jnp.concatenate([w[:, :D_MODEL] * (NA_HEAD_DIM ** -0.5 * LOG2E), w[:, D_MODEL:]],
                                axis=1).astype(BF16)
        prep['na'].append(dict(w_qkv=w_qkv, bias=_na_bias_table(na_rpb[j]),
                               w_out=na_w_out[j].astype(BF16)))
        wr = jnp.pad(moe_router[j], ((0, 0), (0, LANES - N_EXPERTS)))
        wr_hi = wr.astype(BF16)
        wr = jnp.concatenate([wr_hi, (wr - wr_hi.astype(F32)).astype(BF16)], axis=1)
        prep['moe'].append(dict(wr=wr, wg=moe_w_gate[j].astype(BF16), wu=moe_w_up[j].astype(BF16),
                                wd=moe_w_down[j].astype(BF16)))
    return prep


def _trunks(xs, mod_all, norm_gains, prep):
    shapes = [x.shape for x in xs]
    xs = [x.reshape(-1, D_MODEL) for x in xs]
    Ba = shapes[0][0]
    for i in range(DEPTH):
        j = i // 2
        mods = [mod_all[i, :Ba], mod_all[i, Ba:]]
        gains = [norm_gains[i, n].reshape(1, D_MODEL) for n in range(4)]
        for t, (B, L, _) in enumerate(shapes):
            x, mod = xs[t], mods[t]
            if i % 2 == 0:
                p = prep['gla'][j]
                main, z = _proj(x, gains[0], mod, p['w_all'], p['zb'], L,
                                2 * GLA_DK + 2 * GLA_DV, 2 * GLA_DK)
                o = _gla_core(main, z, p['norm'], B, L)
                f = prep['ffn'][j]
                x = _ffn(o, p['w_out'], x, gains[1], gains[2], gains[3], mod,
                         f['wg'], f['wu'], f['wd'], L)
            else:
                p = prep['na'][j]
                qkv = _proj(x, gains[0], mod, p['w_qkv'], None, L, 3 * D_MODEL, 0)
                o = _na_core(qkv, p['bias'], B, L)
                x = _outproj(o, p['w_out'], x, gains[1], mod, L)
            xs[t] = x
        if i % 2 == 1:
            m = prep['moe'][j]
            xs = list(_moe_layer(xs[0], xs[1], gains[2], gains[3], mod_all[i], m['wr'], m['wg'],
                                 m['wu'], m['wd'], shapes[0][1], shapes[1][1]))
    return tuple(x.reshape(s) for x, s in zip(xs, shapes))


def kernel(x_prompt, x_sample, c_prompt, c_sample, ada_w, ada_b, norm_gains, gla_w_in, gla_w_gate_up, gla_b_gate, gla_norm, gla_w_out, na_w_qkv, na_rpb, na_w_out, ffn_w_gate, ffn_w_up, ffn_w_down, moe_router, moe_w_gate, moe_w_up, moe_w_down):
    prep = _prep_weights(gla_w_in, gla_w_gate_up, gla_b_gate, gla_norm, gla_w_out, na_w_qkv, na_rpb,
                         na_w_out, ffn_w_gate, ffn_w_up, ffn_w_down, moe_router, moe_w_gate,
                         moe_w_up, moe_w_down)
    bp, bs = c_prompt.shape[0], c_sample.shape[0]
    rows = -(-(bp + bs) // 8) * 8
    c = jnp.zeros((rows, D_MODEL), F32).at[:bp].set(c_prompt).at[bp:bp + bs].set(c_sample)
    mod = _small_matmul(c, ada_w, ada_b.reshape(DEPTH, 1, 6 * D_MODEL), act=True, col_tile=D_MODEL)
    mod = mod[:, :bp + bs].reshape(DEPTH, bp + bs, 6, D_MODEL)
    return _trunks((x_prompt, x_sample), mod, norm_gains, prep)
```

```python
import functools

import jax
import jax.numpy as jnp
from jax import lax
from jax.experimental import pallas as pl
from jax.experimental.pallas import tpu as pltpu

D_MODEL = 1024
DEPTH = 4
GRID_W = 64
GLA_HEADS = 4
GLA_DK = D_MODEL // 2
GLA_DV = D_MODEL
GLA_DK_HEAD = GLA_DK // GLA_HEADS
GLA_DV_HEAD = GLA_DV // GLA_HEADS
GLA_GATE_RANK = 16
GLA_TAU = 16.0
GLA_CHUNK = 64
NA_HEADS = 16
NA_HEAD_DIM = D_MODEL // NA_HEADS
WIN_ROWS = 8
WIN_COLS = 16
D_FF = 7 * D_MODEL // 2
N_EXPERTS = 8
TOP_K = 2
NORM_EPS = 1e-6
NEG_INF = -1e30
LOG2E = 1.4426950408889634

BF16 = jnp.bfloat16
F32 = jnp.float32

VMEM_LIMIT_BYTES = 56 * 1024 * 1024
LANES = 128
SUBLANES = 8

ROW_TILE = 512
FFN_ROW_TILE = 1024
FFN_COL_TILE = 512
GLA_TIME_BLOCK = 2048
NA_Q_ROWS = 4
NA_K_ROWS = 12
NA_TILES_PER_STEP = 4
MOE_BLOCK = 1024
MOE_ISSUE_STEPS = D_FF // FFN_COL_TILE

def _cparams(*sem):
    return pltpu.CompilerParams(dimension_semantics=sem, vmem_limit_bytes=VMEM_LIMIT_BYTES)


def _norm_mod(x, gain, scale, shift):
    ms = jnp.mean(x * x, axis=-1, keepdims=True)
    y = x * lax.rsqrt(ms + NORM_EPS) * gain
    return y * (1.0 + scale) + shift


def _rms_gain(y, gain):
    ms = jnp.mean(y * y, axis=-1, keepdims=True)
    return y * lax.rsqrt(ms + NORM_EPS) * gain


def _dot(a, b):
    return jnp.dot(a, b, preferred_element_type=F32)


def _dot_nt(a, b):
    return lax.dot_general(a, b, (((1,), (1,)), ((), ())), preferred_element_type=F32)


def _dot_tn(a, b):
    return lax.dot_general(a, b, (((0,), (0,)), ((), ())), preferred_element_type=F32)


def _silu(x):
    return x * (1.0 / (1.0 + jnp.exp(-x)))


def _small_matmul_kernel(a_ref, w_ref, b_ref, o_ref, *, act):
    a = a_ref[...]
    if act:
        a = _silu(a)
    o_ref[0] = jnp.dot(a, w_ref[0], preferred_element_type=F32,
                       precision=lax.Precision.HIGHEST) + b_ref[0]


def _small_matmul(a, w, b, *, act, col_tile):
    M, K = a.shape
    G, _, N = w.shape
    return pl.pallas_call(
        functools.partial(_small_matmul_kernel, act=act),
        grid=(G, N // col_tile),
        in_specs=[pl.BlockSpec((M, K), lambda g, j: (0, 0)),
                  pl.BlockSpec((1, K, col_tile), lambda g, j: (g, 0, j)),
                  pl.BlockSpec((1, 1, col_tile), lambda g, j: (g, 0, j))],
        out_specs=pl.BlockSpec((1, M, col_tile), lambda g, j: (g, 0, j)),
        out_shape=jax.ShapeDtypeStruct((G, M, N), F32),
        compiler_params=_cparams("parallel", "parallel"),
        name="small_matmul",
    )(a, w, b)


def _proj_kernel(x_ref, g_ref, mod_ref, w_ref, zb_ref, o_ref, z_ref, *, n_main, n_z, cn):
    m = mod_ref[0]
    h = _norm_mod(x_ref[...], g_ref[...], m[1:2], m[0:1]).astype(BF16)
    for n in range(0, n_main, cn):
        o_ref[:, n:n + cn] = _dot(h, w_ref[:, n:n + cn]).astype(o_ref.dtype)
    for n in range(0, n_z, cn):
        z_ref[:, n:n + cn] = _dot(h, w_ref[:, n_main + n:n_main + n + cn]) + zb_ref[:, n:n + cn]


def _proj_kernel_noz(x_ref, g_ref, mod_ref, w_ref, o_ref, *, n_main, cn):
    m = mod_ref[0]
    h = _norm_mod(x_ref[...], g_ref[...], m[1:2], m[0:1]).astype(BF16)
    for n in range(0, n_main, cn):
        o_ref[:, n:n + cn] = _dot(h, w_ref[:, n:n + cn]).astype(o_ref.dtype)


def _proj(x, gain, mod, w, zb, L, n_main, n_z):
    T = x.shape[0]
    tm = min(ROW_TILE, L)
    spt = L // tm
    N = n_main + n_z
    in_specs = [pl.BlockSpec((tm, D_MODEL), lambda i: (i, 0)),
                pl.BlockSpec((1, D_MODEL), lambda i: (0, 0)),
                pl.BlockSpec((1, 6, D_MODEL), lambda i: (i // spt, 0, 0)),
                pl.BlockSpec((D_MODEL, N), lambda i: (0, 0))]
    if n_z:
        return pl.pallas_call(
            functools.partial(_proj_kernel, n_main=n_main, n_z=n_z, cn=512),
            grid=(T // tm,),
            in_specs=in_specs + [pl.BlockSpec((1, n_z), lambda i: (0, 0))],
            out_specs=[pl.BlockSpec((tm, n_main), lambda i: (i, 0)),
                       pl.BlockSpec((tm, n_z), lambda i: (i, 0))],
            out_shape=[jax.ShapeDtypeStruct((T, n_main), BF16),
                       jax.ShapeDtypeStruct((T, n_z), F32)],
            compiler_params=_cparams("parallel"),
            name="norm_proj_gla",
        )(x, gain, mod, w, zb)
    return pl.pallas_call(
        functools.partial(_proj_kernel_noz, n_main=n_main, cn=512),
        grid=(T // tm,),
        in_specs=in_specs,
        out_specs=pl.BlockSpec((tm, n_main), lambda i: (i, 0)),
        out_shape=jax.ShapeDtypeStruct((T, n_main), BF16),
        compiler_params=_cparams("parallel"),
        name="norm_proj_na",
    )(x, gain, mod, w)


def _ffn_kernel(o_ref, wo_ref, x_ref, g1_ref, g2_ref, g3_ref, mod_ref, wg_ref, wu_ref, wd_ref,
                y_ref, h_ref, acc_ref):
    f = pl.program_id(1)
    m = mod_ref[0]

    @pl.when(f == 0)
    def _():
        x1 = x_ref[...] + m[2:3] * _rms_gain(_dot(o_ref[...], wo_ref[...]), g1_ref[...])
        y_ref[...] = x1
        h_ref[...] = _norm_mod(x1, g2_ref[...], m[4:5], m[3:4]).astype(BF16)
        acc_ref[...] = jnp.zeros_like(acc_ref)

    h = h_ref[...]
    a = (_silu(_dot(h, wg_ref[...])) * _dot(h, wu_ref[...])).astype(BF16)
    acc_ref[...] += _dot(a, wd_ref[...])

    @pl.when(f == pl.num_programs(1) - 1)
    def _():
        y_ref[...] = y_ref[...] + m[5:6] * _rms_gain(acc_ref[...], g3_ref[...])


def _ffn(o, wo, x, gain1, gain2, gain3, mod, wg, wu, wd, L):
    T = x.shape[0]
    tm = min(FFN_ROW_TILE, L)
    spt = L // tm
    tf = FFN_COL_TILE
    return pl.pallas_call(
        _ffn_kernel,
        grid=(T // tm, D_FF // tf),
        in_specs=[pl.BlockSpec((tm, D_MODEL), lambda i, f: (i, 0)),
                  pl.BlockSpec((D_MODEL, D_MODEL), lambda i, f: (0, 0)),
                  pl.BlockSpec((tm, D_MODEL), lambda i, f: (i, 0)),
                  pl.BlockSpec((1, D_MODEL), lambda i, f: (0, 0)),
                  pl.BlockSpec((1, D_MODEL), lambda i, f: (0, 0)),
                  pl.BlockSpec((1, D_MODEL), lambda i, f: (0, 0)),
                  pl.BlockSpec((1, 6, D_MODEL), lambda i, f: (i // spt, 0, 0)),
                  pl.BlockSpec((D_MODEL, tf), lambda i, f: (0, f)),
                  pl.BlockSpec((D_MODEL, tf), lambda i, f: (0, f)),
                  pl.BlockSpec((tf, D_MODEL), lambda i, f: (f, 0))],
        out_specs=pl.BlockSpec((tm, D_MODEL), lambda i, f: (i, 0)),
        out_shape=jax.ShapeDtypeStruct((T, D_MODEL), F32),
        scratch_shapes=[pltpu.VMEM((tm, D_MODEL), BF16), pltpu.VMEM((tm, D_MODEL), F32)],
        compiler_params=_cparams("parallel", "arbitrary"),
        name="ffn_swiglu",
    )(o, wo, x, gain1, gain2, gain3, mod, wg, wu, wd)


def _gla_block(q_ref, k_ref, v_ref, z_ref, S, *, tb, reverse):
    C = GLA_CHUNK
    chunks = list(range(tb // C))
    order = chunks[::-1] if reverse else chunks
    sl = {c: slice(c * C, (c + 1) * C) for c in chunks}
    row = lax.broadcasted_iota(jnp.int32, (C, C), 0)
    col = lax.broadcasted_iota(jnp.int32, (C, C), 1)
    if reverse:
        tri = col >= row
        allowed = col > row
        i_ref, i_last = C - 1 - C // 2, 0
    else:
        tri = row >= col
        allowed = row >= col
        i_ref, i_last = C // 2, C - 1
    tri = jnp.where(tri, 1.0, 0.0).astype(BF16)
    dk = q_ref.shape[1]
    reps = v_ref.shape[1] // dk

    z = z_ref[...]
    g = (jnp.minimum(z, 0.0) - jnp.log(1.0 + jnp.exp(-jnp.abs(z)))) * (LOG2E / GLA_TAU)
    g_hi = g.astype(BF16)
    g_lo = (g - g_hi.astype(F32)).astype(BF16)
    g_terms = jnp.concatenate([g_hi, g_lo], axis=1)
    G = {}
    for c in chunks:
        t = _dot(tri, g_terms[sl[c]])
        G[c] = t[:, :dk] + t[:, dk:]
    q_rel, k_rel, q_in, k_out, decay = {}, {}, {}, {}, {}
    for c in chunks:
        q = q_ref[sl[c], :].astype(F32)
        k = k_ref[sl[c], :].astype(F32)
        G_ref = G[c][i_ref:i_ref + 1]
        G_last = G[c][i_last:i_last + 1]
        q_rel[c] = (q * jnp.exp2(G[c] - G_ref)).astype(BF16)
        k_rel[c] = (k * jnp.exp2(G_ref - G[c])).astype(BF16)
        q_in[c] = (q * jnp.exp2(G[c])).astype(BF16)
        k_out[c] = (k * jnp.exp2(G_last - G[c])).astype(BF16)
        d = jnp.broadcast_to(jnp.exp2(G_last), (dk, dk)).T
        decay[c] = jnp.concatenate([d] * reps, axis=1)
    scores = {c: jnp.where(allowed, _dot_nt(q_rel[c], k_rel[c]), 0.0).astype(BF16) for c in chunks}
    kv = {c: _dot_tn(k_out[c], v_ref[sl[c], :]) for c in chunks}
    S_in = {}
    for c in order:
        S_in[c] = S.astype(BF16)
        S = decay[c] * S + kv[c]
    o = {c: _dot(scores[c], v_ref[sl[c], :]) + _dot(q_in[c], S_in[c]) for c in chunks}
    return o, S


def _gla_fwd_kernel(q_ref, k_ref, v_ref, z_ref, o_ref, s_ref, *, tb):
    @pl.when(pl.program_id(2) == 0)
    def _():
        s_ref[...] = jnp.zeros_like(s_ref)

    o, S = _gla_block(q_ref, k_ref, v_ref, z_ref, s_ref[...], tb=tb, reverse=False)
    for c, oc in o.items():
        o_ref[c * GLA_CHUNK:(c + 1) * GLA_CHUNK, :] = oc
    s_ref[...] = S


def _gla_bwd_kernel(q_ref, k_ref, v_ref, z_ref, of_ref, r_ref, gain_ref, o_ref, s_ref, *, tb):
    @pl.when(pl.program_id(2) == 0)
    def _():
        s_ref[...] = jnp.zeros_like(s_ref)

    o, S = _gla_block(q_ref, k_ref, v_ref, z_ref, s_ref[...], tb=tb, reverse=True)
    for c, oc in o.items():
        sl = slice(c * GLA_CHUNK, (c + 1) * GLA_CHUNK)
        oc = _rms_gain(oc + of_ref[sl, :], gain_ref[...])
        o_ref[sl, :] = (oc * _silu(r_ref[sl, :].astype(F32))).astype(o_ref.dtype)
    s_ref[...] = S


def _gla_core(main, z, gla_norm, B, L):
    T = B * L
    tb = min(GLA_TIME_BLOCK, L)
    nt = L // tb
    H = GLA_HEADS
    dk, dv = GLA_DK_HEAD, GLA_DV_HEAD
    grid = (B, H, nt)
    scratch = [pltpu.VMEM((dk, dv), F32)]

    def fwd_t(b, h, t):
        return b * nt + t

    def bwd_t(b, h, t):
        return b * nt + (nt - 1 - t)

    o_fwd = pl.pallas_call(
        functools.partial(_gla_fwd_kernel, tb=tb),
        grid=grid,
        in_specs=[pl.BlockSpec((tb, dk), lambda b, h, t: (fwd_t(b, h, t), h)),
                  pl.BlockSpec((tb, dk), lambda b, h, t: (fwd_t(b, h, t), H + h)),
                  pl.BlockSpec((tb, dv), lambda b, h, t: (fwd_t(b, h, t), H + h)),
                  pl.BlockSpec((tb, dk), lambda b, h, t: (fwd_t(b, h, t), h))],
        out_specs=pl.BlockSpec((tb, dv), lambda b, h, t: (fwd_t(b, h, t), h)),
        out_shape=jax.ShapeDtypeStruct((T, GLA_DV), F32),
        scratch_shapes=scratch,
        compiler_params=_cparams("parallel", "parallel", "arbitrary"),
        name="gla_forward",
    )(main, main, main, z)
    return pl.pallas_call(
        functools.partial(_gla_bwd_kernel, tb=tb),
        grid=grid,
        in_specs=[pl.BlockSpec((tb, dk), lambda b, h, t: (bwd_t(b, h, t), h)),
                  pl.BlockSpec((tb, dk), lambda b, h, t: (bwd_t(b, h, t), H + h)),
                  pl.BlockSpec((tb, dv), lambda b, h, t: (bwd_t(b, h, t), H + h)),
                  pl.BlockSpec((tb, dk), lambda b, h, t: (bwd_t(b, h, t), H + h)),
                  pl.BlockSpec((tb, dv), lambda b, h, t: (bwd_t(b, h, t), h)),
                  pl.BlockSpec((tb, dv), lambda b, h, t: (bwd_t(b, h, t), 2 * H + h)),
                  pl.BlockSpec((1, dv), lambda b, h, t: (0, h))],
        out_specs=pl.BlockSpec((tb, dv), lambda b, h, t: (bwd_t(b, h, t), h)),
        out_shape=jax.ShapeDtypeStruct((T, GLA_DV), BF16),
        scratch_shapes=scratch,
        compiler_params=_cparams("parallel", "parallel", "arbitrary"),
        name="gla_backward_finish",
    )(main, main, main, z, o_fwd, main, gla_norm)


def _na_bias_table(rpb):
    cols = jnp.arange(GRID_W)
    col_start = jnp.clip(cols - WIN_COLS // 2, 0, GRID_W - WIN_COLS)
    in_win = (cols[None, :] >= col_start[:, None]) & (cols[None, :] < col_start[:, None] + WIN_COLS)
    dc_idx = jnp.clip(cols[None, :] - cols[:, None] + WIN_COLS - 1, 0, 2 * WIN_COLS - 2)
    pick_c = (dc_idx[:, :, None] == jnp.arange(2 * WIN_COLS - 1)).astype(F32)
    a = jnp.arange(NA_Q_ROWS)
    j = jnp.arange(NA_K_ROWS)
    tables = []
    for delta, win_start in ((0, jnp.zeros_like(a)), (4, a), (8, jnp.full_like(a, 4))):
        live_row = (j[None, :] >= win_start[:, None]) & (j[None, :] < win_start[:, None] + WIN_ROWS)
        dr_idx = jnp.clip(j[None, :] - delta - a[:, None] + WIN_ROWS - 1, 0, 2 * WIN_ROWS - 2)
        pick_r = (dr_idx[:, :, None] == jnp.arange(2 * WIN_ROWS - 1)).astype(F32)
        bias = jnp.einsum('hrc,ajr,qkc->haqjk', rpb.astype(F32), pick_r, pick_c,
                          precision=lax.Precision.HIGHEST) * LOG2E
        live = live_row[:, None, :, None] & in_win[None, :, None, :]
        tables.append(jnp.where(live[None], bias, NEG_INF))
    t = jnp.stack(tables)
    return t.reshape(3, NA_HEADS, NA_Q_ROWS * GRID_W, NA_K_ROWS * GRID_W)


def _na_kernel(q_ref, k_ref, v_ref, bias_ref, o_ref, *, rows):
    ntile = rows // NA_Q_ROWS
    tq = NA_Q_ROWS * GRID_W
    tk = NA_K_ROWS * GRID_W
    lane = lax.broadcasted_iota(jnp.int32, (1, LANES), 1)
    first_head = lane < NA_HEAD_DIM

    def body(tt, carry):
        q0, geo, k, v, s = {}, {}, {}, {}, {}
        units = [(u, hh) for u in range(NA_TILES_PER_STEP) for hh in range(2)]
        for u in range(NA_TILES_PER_STEP):
            t = NA_TILES_PER_STEP * tt + u
            kr0 = jnp.clip(NA_Q_ROWS * t - 4, 0, rows - NA_K_ROWS)
            geo[u] = jnp.where(t == 0, 0, jnp.where(t == ntile - 1, 2, 1))
            q0[u] = pl.multiple_of(t * tq, tq)
            k0 = pl.multiple_of(kr0 * GRID_W, GRID_W)
            q = q_ref[pl.ds(q0[u], tq), :]
            k[u] = k_ref[pl.ds(k0, tk), :]
            v[u] = v_ref[pl.ds(k0, tk), :]
            for hh in range(2):
                sel = first_head if hh == 0 else jnp.logical_not(first_head)
                s[u, hh] = _dot_nt(jnp.where(sel, q, jnp.zeros_like(q)), k[u])
        p, inv_l, o = {}, {}, {}
        for u, hh in units:
            sb = s[u, hh] + bias_ref[geo[u], hh]
            e = jnp.exp2(sb - jnp.max(sb, axis=-1, keepdims=True))
            inv_l[u, hh] = 1.0 / jnp.sum(e, axis=-1, keepdims=True)
            p[u, hh] = e.astype(BF16)
        for u, hh in units:
            o[u, hh] = _dot(p[u, hh], v[u]) * inv_l[u, hh]
        for u in range(NA_TILES_PER_STEP):
            o_ref[pl.ds(q0[u], tq), :] = jnp.where(first_head, o[u, 0], o[u, 1]).astype(o_ref.dtype)
        return carry

    lax.fori_loop(0, ntile // NA_TILES_PER_STEP, body, 0)


def _na_core(qkv, bias, B, L):
    T = B * L
    rows = L // GRID_W
    npair = NA_HEADS // 2
    tq = NA_Q_ROWS * GRID_W
    tk = NA_K_ROWS * GRID_W
    assert (rows // NA_Q_ROWS) % NA_TILES_PER_STEP == 0 and rows >= NA_K_ROWS
    return pl.pallas_call(
        functools.partial(_na_kernel, rows=rows),
        grid=(npair, B),
        in_specs=[pl.BlockSpec((L, LANES), lambda p, b: (b, p)),
                  pl.BlockSpec((L, LANES), lambda p, b: (b, npair + p)),
                  pl.BlockSpec((L, LANES), lambda p, b: (b, 2 * npair + p)),
                  pl.BlockSpec((3, 2, tq, tk), lambda p, b: (0, p, 0, 0))],
        out_specs=pl.BlockSpec((L, LANES), lambda p, b: (b, p)),
        out_shape=jax.ShapeDtypeStruct((T, D_MODEL), BF16),
        compiler_params=_cparams("parallel", "parallel"),
        name="neighbourhood_attention",
    )(qkv, qkv, qkv, bias)


def _router_kernel(oa_ref, ob_ref, wo_ref, xa_ref, xb_ref, g1_ref, g_ref, mod_ref, wr_ref,
                   x1a_ref, x1b_ref, h_ref, route_ref, cnt_ref, carry_ref, *, na):
    i = pl.program_id(0)

    @pl.when(i == 0)
    def _():
        carry_ref[...] = jnp.zeros_like(carry_ref)

    m = mod_ref[0]
    first = i < na
    y = _dot(jnp.where(first, oa_ref[...], ob_ref[...]), wo_ref[...])
    x = jnp.where(first, xa_ref[...], xb_ref[...]) + m[2:3] * _rms_gain(y, g1_ref[...])

    @pl.when(first)
    def _():
        x1a_ref[...] = x

    @pl.when(jnp.logical_not(first))
    def _():
        x1b_ref[...] = x

    h = _norm_mod(x, g_ref[...], m[4:5], m[3:4])
    tm = h.shape[0]
    for s in range(SUBLANES):
        h_ref[pl.ds(s, tm, stride=SUBLANES), :] = h[:, s * LANES:(s + 1) * LANES]
    h_hi = h.astype(BF16)
    h_lo = (h - h_hi.astype(F32)).astype(BF16)
    both = _dot(h_hi, wr_ref[...])
    logits = both[:, :LANES] + both[:, LANES:] + _dot(h_lo, wr_ref[:, :LANES])
    lane = lax.broadcasted_iota(jnp.int32, logits.shape, 1)
    valid = lane < N_EXPERTS
    l1 = jnp.where(valid, logits, -jnp.inf)
    m1 = jnp.max(l1, axis=-1, keepdims=True)
    i1 = jnp.min(jnp.where(l1 == m1, lane, LANES), axis=-1, keepdims=True)
    l2 = jnp.where(lane == i1, -jnp.inf, l1)
    m2 = jnp.max(l2, axis=-1, keepdims=True)
    i2 = jnp.min(jnp.where(l2 == m2, lane, LANES), axis=-1, keepdims=True)
    e = jnp.exp(m2 - m1)
    g1 = 1.0 / (1.0 + e)
    g2 = e * g1
    chosen = (lane == i1) | (lane == i2)
    carry_ref[...] = carry_ref[...] + jnp.sum(jnp.where(chosen, 1.0, 0.0), axis=0, keepdims=True)
    cnt_ref[...] = carry_ref[...]
    route = jnp.where(lane == 0, i1.astype(F32),
                      jnp.where(lane == 1, i2.astype(F32),
                                jnp.where(lane == 2, g1, jnp.where(lane == 3, g2, 0.0))))
    route_ref[...] = route


class _TwoTrunks:
    def __init__(self, xa, xb, La, Lb, tm):
        self.na = xa.shape[0] // tm
        self.nb = xb.shape[0] // tm
        self.T = xa.shape[0] + xb.shape[0]
        self.tm = tm
        seqs_a = xa.shape[0] // La
        tiles_per_seq_a = La // tm
        tiles_per_seq_b = Lb // tm
        na = self.na
        self.a_tile = lambda i: (jnp.minimum(i, na - 1), 0)
        self.b_tile = lambda i: (jnp.maximum(i - na, 0), 0)
        self.seq = lambda i: (jnp.where(i < na, i // tiles_per_seq_a,
                                        seqs_a + (i - na) // tiles_per_seq_b), 0, 0)


def _router(oa, ob, wo, xa, xb, gain1, gain2, mod, wr, La, Lb):
    tm = min(ROW_TILE, La, Lb)
    tt = _TwoTrunks(xa, xb, La, Lb, tm)
    T = tt.T
    return pl.pallas_call(
        functools.partial(_router_kernel, na=tt.na),
        grid=(T // tm,),
        in_specs=[pl.BlockSpec((tm, D_MODEL), tt.a_tile),
                  pl.BlockSpec((tm, D_MODEL), tt.b_tile),
                  pl.BlockSpec((D_MODEL, D_MODEL), lambda i: (0, 0)),
                  pl.BlockSpec((tm, D_MODEL), tt.a_tile),
                  pl.BlockSpec((tm, D_MODEL), tt.b_tile),
                  pl.BlockSpec((1, D_MODEL), lambda i: (0, 0)),
                  pl.BlockSpec((1, D_MODEL), lambda i: (0, 0)),
                  pl.BlockSpec((1, 6, D_MODEL), tt.seq),
                  pl.BlockSpec((D_MODEL, 2 * LANES), lambda i: (0, 0))],
        out_specs=[pl.BlockSpec((tm, D_MODEL), tt.a_tile),
                   pl.BlockSpec((tm, D_MODEL), tt.b_tile),
                   pl.BlockSpec((tm * SUBLANES, LANES), lambda i: (i, 0)),
                   pl.BlockSpec((tm, LANES), lambda i: (i, 0)),
                   pl.BlockSpec((SUBLANES, LANES), lambda i: (0, 0))],
        out_shape=[jax.ShapeDtypeStruct(xa.shape, F32),
                   jax.ShapeDtypeStruct(xb.shape, F32),
                   jax.ShapeDtypeStruct((T * SUBLANES, LANES), F32),
                   jax.ShapeDtypeStruct((T, LANES), F32),
                   jax.ShapeDtypeStruct((SUBLANES, LANES), F32)],
        scratch_shapes=[pltpu.VMEM((SUBLANES, LANES), F32)],
        compiler_params=_cparams("arbitrary"),
        name="moe_router",
    )(oa, ob, wo, xa, xb, gain1, gain2, mod, wr)


def _moe_kernel(be_ref, nused_ref, srcx_hbm, dstx_hbm, h_hbm, wg_ref, wu_ref, wd_ref, out_hbm,
                src_smem, dst_smem, xbuf, xb16, acc_ref, ybuf,
                src_sem, dst_sem, gat_sem, sca_sem, *, bm):
    b = pl.program_id(0)
    f = pl.program_id(1)
    nb = pl.num_programs(0)
    nf = pl.num_programs(1)
    slot = b % 2
    nxt = 1 - slot
    group = -(-bm // MOE_ISSUE_STEPS)
    always = bm - (MOE_ISSUE_STEPS - 1) * group

    def src_copy(r, s):
        return pltpu.make_async_copy(srcx_hbm.at[r], src_smem.at[s], src_sem.at[s])

    def dst_copy(r, s):
        return pltpu.make_async_copy(dstx_hbm.at[r], dst_smem.at[s], dst_sem.at[s])

    def tile(ref, i):
        return ref.at[pl.ds(pl.multiple_of(i * SUBLANES, SUBLANES), SUBLANES)]

    def gather_row(i, s):
        return pltpu.make_async_copy(tile(h_hbm, src_smem[s, i]), tile(xbuf.at[s], i), gat_sem.at[s])

    def scatter_row(i, s):
        return pltpu.make_async_copy(tile(ybuf, i), tile(out_hbm, dst_smem[s, i]), sca_sem)

    def wait_gather(s):
        pltpu.make_async_copy(h_hbm.at[pl.ds(0, bm * SUBLANES)], xbuf.at[s], gat_sem.at[s]).wait()

    def wait_scatter():
        pltpu.make_async_copy(ybuf, out_hbm.at[pl.ds(0, bm * SUBLANES)], sca_sem).wait()

    def for_all_rows(start_row):
        def issue(i, c):
            start_row(i)
            return c
        lax.fori_loop(0, bm, issue, 0)

    @pl.when(f == 0)
    def _():
        @pl.when(b == 0)
        def _():
            src_copy(0, 0).start()
            src_copy(1, 1).start()
            dst_copy(0, 0).start()
            src_copy(0, 0).wait()
            for_all_rows(lambda i: gather_row(i, 0).start())
            ybuf[...] = jnp.zeros_like(ybuf)

        src_copy(b + 1, nxt).wait()
        dst_copy(b, slot).wait()
        wait_gather(slot)
        for s in range(SUBLANES):
            xb16[:, s * LANES:(s + 1) * LANES] = xbuf[slot, pl.ds(s, bm, stride=SUBLANES), :].astype(BF16)
        acc_ref[...] = jnp.zeros_like(acc_ref)

    base = f * group
    used = b < nused_ref[0]

    @pl.when(used)
    def _():
        for j in range(always):
            gather_row(base + j, nxt).start()
            scatter_row(base + j, slot).start()
        x = xb16[...]
        a = (_silu(_dot(x, wg_ref[0])) * _dot(x, wu_ref[0])).astype(BF16)
        acc_ref[...] += _dot(a, wd_ref[0])

    @pl.when(jnp.logical_not(used))
    def _():
        def issue(j, c):
            gather_row(base + j, nxt).start()
            scatter_row(base + j, slot).start()
            return c
        lax.fori_loop(0, always, issue, 0)

    @pl.when(f < nf - 1)
    def _():
        for j in range(always, group):
            gather_row(base + j, nxt).start()
            scatter_row(base + j, slot).start()

    @pl.when(f == nf - 1)
    def _():
        wait_scatter()
        for s in range(SUBLANES):
            ybuf[pl.ds(s, bm, stride=SUBLANES), :] = acc_ref[:, s * LANES:(s + 1) * LANES]
        dst_copy(b + 1, nxt).start()

        @pl.when(b + 2 <= nb)
        def _():
            src_copy(b + 2, slot).start()

        @pl.when(b == nb - 1)
        def _():
            dst_copy(b + 1, nxt).wait()
            for_all_rows(lambda i: scatter_row(i, nxt).start())
            wait_scatter()
            wait_gather(nxt)


def _moe_experts(block_e, nused, srcx, dstx, h, wg, wu, wd, n_rows_out):
    n_blocks = srcx.shape[0] - 1
    bm = srcx.shape[1]
    tf = FFN_COL_TILE
    assert D_FF // tf == MOE_ISSUE_STEPS
    grid_spec = pltpu.PrefetchScalarGridSpec(
        num_scalar_prefetch=2,
        grid=(n_blocks, D_FF // tf),
        in_specs=[pl.BlockSpec(memory_space=pl.ANY),
                  pl.BlockSpec(memory_space=pl.ANY),
                  pl.BlockSpec(memory_space=pl.ANY),
                  pl.BlockSpec((1, D_MODEL, tf), lambda b, f, be, nu: (be[b], 0, f)),
                  pl.BlockSpec((1, D_MODEL, tf), lambda b, f, be, nu: (be[b], 0, f)),
                  pl.BlockSpec((1, tf, D_MODEL), lambda b, f, be, nu: (be[b], f, 0))],
        out_specs=pl.BlockSpec(memory_space=pl.ANY),
        scratch_shapes=[pltpu.SMEM((2, bm), jnp.int32),
                        pltpu.SMEM((2, bm), jnp.int32),
                        pltpu.VMEM((2, bm * SUBLANES, LANES), F32),
                        pltpu.VMEM((bm, D_MODEL), BF16),
                        pltpu.VMEM((bm, D_MODEL), F32),
                        pltpu.VMEM((bm * SUBLANES, LANES), F32),
                        pltpu.SemaphoreType.DMA((2,)),
                        pltpu.SemaphoreType.DMA((2,)),
                        pltpu.SemaphoreType.DMA((2,)),
                        pltpu.SemaphoreType.DMA(())],
    )
    return pl.pallas_call(
        functools.partial(_moe_kernel, bm=bm),
        grid_spec=grid_spec,
        out_shape=jax.ShapeDtypeStruct((n_rows_out * SUBLANES, LANES), F32),
        compiler_params=_cparams("arbitrary", "arbitrary"),
        name="moe_experts",
    )(block_e, nused, srcx, dstx, h, wg, wu, wd)


def _combine_kernel(y1_ref, y2_ref, route_ref, xa_ref, xb_ref, g_ref, mod_ref, oa_ref, ob_ref, y_ref,
                    *, na):
    i = pl.program_id(0)
    r = route_ref[...]
    tm = r.shape[0]
    for s in range(SUBLANES):
        rows = pl.ds(s, tm, stride=SUBLANES)
        y_ref[:, s * LANES:(s + 1) * LANES] = r[:, 2:3] * y1_ref[rows, :] + r[:, 3:4] * y2_ref[rows, :]
    upd = mod_ref[0][5:6] * _rms_gain(y_ref[...], g_ref[...])

    @pl.when(i < na)
    def _():
        oa_ref[...] = xa_ref[...] + upd

    @pl.when(i >= na)
    def _():
        ob_ref[...] = xb_ref[...] + upd


def _moe_combine(y, route, xa, xb, gain, mod, La, Lb):
    tm = min(ROW_TILE, La, Lb)
    tt = _TwoTrunks(xa, xb, La, Lb, tm)
    nt = tt.T // tm
    return pl.pallas_call(
        functools.partial(_combine_kernel, na=tt.na),
        grid=(nt,),
        in_specs=[pl.BlockSpec((tm * SUBLANES, LANES), lambda i: (i, 0)),
                  pl.BlockSpec((tm * SUBLANES, LANES), lambda i: (nt + i, 0)),
                  pl.BlockSpec((tm, LANES), lambda i: (i, 0)),
                  pl.BlockSpec((tm, D_MODEL), tt.a_tile),
                  pl.BlockSpec((tm, D_MODEL), tt.b_tile),
                  pl.BlockSpec((1, D_MODEL), lambda i: (0, 0)),
                  pl.BlockSpec((1, 6, D_MODEL), tt.seq)],
        out_specs=[pl.BlockSpec((tm, D_MODEL), tt.a_tile),
                   pl.BlockSpec((tm, D_MODEL), tt.b_tile)],
        out_shape=[jax.ShapeDtypeStruct(xa.shape, F32), jax.ShapeDtypeStruct(xb.shape, F32)],
        scratch_shapes=[pltpu.VMEM((tm, D_MODEL), F32)],
        compiler_params=_cparams("arbitrary"),
        name="moe_combine_residual",
    )(y, y, route, xa, xb, gain, mod)


def _moe_plan(route, counts, T, bm):
    n_assign = TOP_K * T
    n_blocks = n_assign // bm + N_EXPERTS
    experts = jnp.arange(N_EXPERTS, dtype=jnp.int32)
    e_flat = jnp.concatenate([route[:, k] for k in range(TOP_K)]).astype(jnp.int32)
    _, order = lax.sort_key_val(e_flat, jnp.arange(n_assign, dtype=jnp.int32))
    counts = counts[0, :N_EXPERTS].astype(jnp.int32)
    padded = ((counts + bm - 1) // bm) * bm
    start = jnp.cumsum(counts) - counts
    pend = jnp.cumsum(padded)
    pstart = pend - padded
    pos = jnp.arange(n_blocks * bm, dtype=jnp.int32)
    pe = jnp.minimum(jnp.sum(pos[:, None] >= pend[None, :], axis=1), N_EXPERTS - 1).astype(jnp.int32)
    mine = pe[:, None] == experts[None, :]

    def of_expert(table):
        return jnp.sum(jnp.where(mine, table[None, :], 0), axis=1).astype(jnp.int32)

    rank = pos - of_expert(pstart)
    valid = rank < of_expert(counts)
    a = jnp.take(order, jnp.clip(of_expert(start) + rank, 0, n_assign - 1))
    spare = (n_assign + pos % bm).astype(jnp.int32)
    dst = jnp.where(valid, a, spare)
    src = jnp.where(valid, a % T, 0)
    block_e = pe[::bm]
    srcx = jnp.concatenate([src.reshape(n_blocks, bm), jnp.zeros((1, bm), jnp.int32)], axis=0)
    dstx = jnp.concatenate([spare[:bm].reshape(1, bm), dst.reshape(n_blocks, bm)], axis=0)
    nused = (pend[N_EXPERTS - 1] // bm).astype(jnp.int32).reshape(1)
    return block_e, nused, srcx, dstx


def _moe_layer(oa, ob, wo, xa, xb, gain1, gain2, gain3, mod, wr, wg, wu, wd, La, Lb):
    T = xa.shape[0] + xb.shape[0]
    bm = min(MOE_BLOCK, T)
    xa, xb, h, route, counts = _router(oa, ob, wo, xa, xb, gain1, gain2, mod, wr, La, Lb)
    block_e, nused, srcx, dstx = _moe_plan(route, counts, T, bm)
    y = _moe_experts(block_e, nused, srcx, dstx, h, wg, wu, wd, TOP_K * T + bm)
    return _moe_combine(y, route, xa, xb, gain3, mod, La, Lb)


def _prep_weights(gla_w_in, gla_w_gate_up, gla_b_gate, gla_norm, gla_w_out, na_w_qkv, na_rpb,
                  na_w_out, ffn_w_gate, ffn_w_up, ffn_w_down, moe_router, moe_w_gate, moe_w_up,
                  moe_w_down):
    n_a = gla_w_in.shape[0]
    n_b = na_w_qkv.shape[0]
    R = GLA_GATE_RANK
    n_qkvr = 2 * GLA_DK + 2 * GLA_DV
    prep = {'gla': [], 'na': [], 'ffn': [], 'moe': []}
    for j in range(n_a):
        w = gla_w_in[j]
        w1 = jnp.concatenate([w[:, n_qkvr:n_qkvr + R], w[:, n_qkvr + R:]], axis=0)
        w1 = jnp.pad(w1, ((0, 0), (0, LANES - R)))
        w2 = jnp.pad(gla_w_gate_up[j], ((0, 0), (0, LANES - R), (0, 0)))
        fold = _small_matmul(w1, w2, jnp.zeros((2, 1, GLA_DK), F32),
                             act=False, col_tile=GLA_DK)
        w_z = jnp.concatenate([fold[0, :D_MODEL], fold[1, D_MODEL:]], axis=1)
        w_q = w[:, :GLA_DK] * (GLA_DK_HEAD ** -0.5)
        w_all = jnp.concatenate([w_q, w[:, GLA_DK:n_qkvr], w_z], axis=1).astype(BF16)
        prep['gla'].append(dict(w_all=w_all, zb=gla_b_gate[j].reshape(1, 2 * GLA_DK),
                                norm=gla_norm[j].reshape(1, GLA_DV),
                                w_out=gla_w_out[j].astype(BF16)))
        prep['ffn'].append(dict(wg=ffn_w_gate[j].astype(BF16), wu=ffn_w_up[j].astype(BF16),
                                wd=ffn_w_down[j].astype(BF16)))
    for j in range(n_b):
        w = na_w_qkv[j]
        w_qkv = jnp.concatenate([w[:, :D_MODEL] * (NA_HEAD_DIM ** -0.5 * LOG2E), w[:, D_MODEL:]],
                                axis=1).astype(BF16)
        prep['na'].append(dict(w_qkv=w_qkv, bias=_na_bias_table(na_rpb[j]),
                               w_out=na_w_out[j].astype(BF16)))
        wr = jnp.pad(moe_router[j], ((0, 0), (0, LANES - N_EXPERTS)))
        wr_hi = wr.astype(BF16)
        wr = jnp.concatenate([wr_hi, (wr - wr_hi.astype(F32)).astype(BF16)], axis=1)
        prep['moe'].append(dict(wr=wr, wg=moe_w_gate[j].astype(BF16), wu=moe_w_up[j].astype(BF16),
                                wd=moe_w_down[j].astype(BF16)))
    return prep


def _trunks(xs, mod_all, norm_gains, prep):
    shapes = [x.shape for x in xs]
    xs = [x.reshape(-1, D_MODEL) for x in xs]
    Ba = shapes[0][0]
    for i in range(DEPTH):
        j = i // 2
        mods = [mod_all[i, :Ba], mod_all[i, Ba:]]
        gains = [norm_gains[i, n].reshape(1, D_MODEL) for n in range(4)]
        mixed = [None, None]
        for t, (B, L, _) in enumerate(shapes):
            x, mod = xs[t], mods[t]
            if i % 2 == 0:
                p = prep['gla'][j]
                main, z = _proj(x, gains[0], mod, p['w_all'], p['zb'], L,
                                2 * GLA_DK + 2 * GLA_DV, 2 * GLA_DK)
                o = _gla_core(main, z, p['norm'], B, L)
                f = prep['ffn'][j]
                x = _ffn(o, p['w_out'], x, gains[1], gains[2], gains[3], mod,
                         f['wg'], f['wu'], f['wd'], L)
            else:
                p = prep['na'][j]
                qkv = _proj(x, gains[0], mod, p['w_qkv'], None, L, 3 * D_MODEL, 0)
                mixed[t] = _na_core(qkv, p['bias'], B, L)
            xs[t] = x
        if i % 2 == 1:
            m = prep['moe'][j]
            xs = list(_moe_layer(mixed[0], mixed[1], prep['na'][j]['w_out'], xs[0], xs[1],
                                 gains[1], gains[2], gains[3], mod_all[i], m['wr'], m['wg'],
                                 m['wu'], m['wd'], shapes[0][1], shapes[1][1]))
    return tuple(x.reshape(s) for x, s in zip(xs, shapes))


def kernel(x_prompt, x_sample, c_prompt, c_sample, ada_w, ada_b, norm_gains, gla_w_in, gla_w_gate_up, gla_b_gate, gla_norm, gla_w_out, na_w_qkv, na_rpb, na_w_out, ffn_w_gate, ffn_w_up, ffn_w_down, moe_router, moe_w_gate, moe_w_up, moe_w_down):
    prep = _prep_weights(gla_w_in, gla_w_gate_up, gla_b_gate, gla_norm, gla_w_out, na_w_qkv, na_rpb,
                         na_w_out, ffn_w_gate, ffn_w_up, ffn_w_down, moe_router, moe_w_gate,
                         moe_w_up, moe_w_down)
    bp, bs = c_prompt.shape[0], c_sample.shape[0]
    rows = -(-(bp + bs) // 8) * 8
    c = jnp.zeros((rows, D_MODEL), F32).at[:bp].set(c_prompt).at[bp:bp + bs].set(c_sample)
    mod = _small_matmul(c, ada_w, ada_b.reshape(DEPTH, 1, 6 * D_MODEL), act=True, col_tile=D_MODEL)
    mod = mod[:, :bp + bs].reshape(DEPTH, bp + bs, 6, D_MODEL)
    return _trunks((x_prompt, x_sample), mod, norm_gains, prep)
```
